```python
import jax, jax.numpy as jnp
from jax import lax
import numpy as np

D_MODEL = 1024
BATCH = 1
SEQ = 16384
DEPTH = 2

HEAD_DIM = 64
ATTN_HEADS = 8
ATTN_WIDTH = ATTN_HEADS * HEAD_DIM
GLA_HEADS = 4
GLA_DK = 64
GLA_DV = 128
GLA_K_WIDTH = GLA_HEADS * GLA_DK
GLA_V_WIDTH = GLA_HEADS * GLA_DV
GLA_GATE_RANK = 16
GLA_TAU = 16.0
GLA_CHUNK = 64
MIX_WIDTH = ATTN_WIDTH + GLA_V_WIDTH
IN_WIDTH = 3 * ATTN_WIDTH + 2 * GLA_K_WIDTH + 2 * GLA_V_WIDTH + GLA_GATE_RANK
D_FF = -(-8 * D_MODEL // (3 * 256)) * 256
ROPE_THETA = 500000.0
ROPE_DIM = HEAD_DIM // 4
DILATED_PATTERNS = ((128, 1), (512, 4), (2048, 16))
N_PATTERNS = len(DILATED_PATTERNS)
N_KEYS = DILATED_PATTERNS[0][0] // DILATED_PATTERNS[0][1] + 1
Q_BLOCK = 128
RMS_EPS = 1e-6

kernel_name = "hymba_dilated_gla_hybrid"

F32 = jnp.float32


def rmsnorm(x, g):
    xf = x.astype(F32)
    y = xf * lax.rsqrt(jnp.mean(xf * xf, axis=-1, keepdims=True) + RMS_EPS)
    return (y * g.astype(F32)).astype(x.dtype)


def rope_partial(x, pos):
    half = ROPE_DIM // 2
    inv = ROPE_THETA ** (-(jnp.arange(half, dtype=F32) * 2.0) / ROPE_DIM)
    ang = pos.astype(F32)[:, None] * inv[None, :]
    cos = jnp.cos(ang)[None, :, None, :]
    sin = jnp.sin(ang)[None, :, None, :]
    xr = x[..., :ROPE_DIM].astype(F32)
    x1, x2 = xr[..., :half], xr[..., half:]
    rot = jnp.concatenate([x1 * cos - x2 * sin, x2 * cos + x1 * sin], axis=-1).astype(x.dtype)
    return jnp.concatenate([rot, x[..., ROPE_DIM:]], axis=-1)


def dilated_attention(q, k, v):
    B, S, H, hd = q.shape
    nblk = S // Q_BLOCK
    dists = jnp.concatenate([d * jnp.arange(w // d + 1) for (w, d) in DILATED_PATTERNS])
    q_blocks = q.reshape(B, nblk, Q_BLOCK, H, hd).transpose(1, 0, 2, 3, 4)
    starts = jnp.arange(nblk) * Q_BLOCK
    scale = hd ** -0.5

    def block(args):
        q_blk, start = args
        pos = start + jnp.arange(Q_BLOCK)
        idx = pos[:, None] - dists[None, :]
        valid = idx >= 0
        idx_c = jnp.maximum(idx, 0)
        kg = jnp.take(k, idx_c, axis=1).astype(F32)
        vg = jnp.take(v, idx_c, axis=1).astype(F32)
        s = jnp.einsum('bqhd,bqnhd->bhqn', q_blk.astype(F32), kg) * scale
        s = jnp.where(valid[None, None], s, -jnp.inf)
        s = s.reshape(B, H, Q_BLOCK, N_PATTERNS, N_KEYS)
        lse = jax.nn.logsumexp(s, axis=-1, keepdims=True)
        p = jnp.exp(s - lse)
        o_pat = jnp.einsum('bhqpn,bqpnhd->bhqpd', p,
                           vg.reshape(B, Q_BLOCK, N_PATTERNS, N_KEYS, H, hd))
        w = jax.nn.softmax(lse, axis=-2)
        o = jnp.sum(w * o_pat, axis=-2)
        return o.transpose(0, 2, 1, 3).astype(q.dtype)

    out = lax.map(block, (q_blocks, starts))
    return out.transpose(1, 0, 2, 3, 4).reshape(B, S, H, hd)


def gla_chunked(q, k, v, g):
    B, S, H, dk = q.shape
    dv = v.shape[-1]
    C = GLA_CHUNK
    nc = S // C

    def chunks(t):
        return t.astype(F32).reshape(B, nc, C, H, t.shape[-1]).transpose(1, 0, 3, 2, 4)

    causal = jnp.tril(jnp.ones((C, C), dtype=bool))

    def step(state, inp):
        qc, kc, vc, gc = inp
        b = jnp.cumsum(gc, axis=-2)
        o_inter = jnp.einsum('bhtk,bhkv->bhtv', qc * jnp.exp(b), state)
        diff = b[:, :, :, None, :] - b[:, :, None, :, :]
        decay = jnp.exp(jnp.where(causal[None, None, :, :, None], diff, -jnp.inf))
        attn = jnp.einsum('bhtk,bhsk,bhtsk->bhts', qc, kc, decay)
        o_intra = jnp.einsum('bhts,bhsv->bhtv', attn, vc)
        b_last = b[:, :, -1:, :]
        state = jnp.exp(b_last[:, :, 0, :])[..., None] * state + \
            jnp.einsum('bhsk,bhsv->bhkv', kc * jnp.exp(b_last - b), vc)
        return state, o_intra + o_inter

    s0 = jnp.zeros((B, H, dk, dv), F32)
    _, o = lax.scan(step, s0, (chunks(q) * (dk ** -0.5), chunks(k), chunks(v), chunks(g)))
    return o.transpose(1, 0, 3, 2, 4).reshape(B, S, H, dv).astype(v.dtype)


def hybrid_mixer(h, w_in, gla_w_gate_up, gla_b_gate, gla_out_norm, attn_out_norm, w_out):
    B, S, _ = h.shape
    proj = h @ w_in
    cuts = [ATTN_WIDTH, 2 * ATTN_WIDTH, 3 * ATTN_WIDTH,
            3 * ATTN_WIDTH + GLA_K_WIDTH, 3 * ATTN_WIDTH + 2 * GLA_K_WIDTH,
            3 * ATTN_WIDTH + 2 * GLA_K_WIDTH + GLA_V_WIDTH,
            3 * ATTN_WIDTH + 2 * GLA_K_WIDTH + 2 * GLA_V_WIDTH]
    aq, ak, av, gq, gk, gv, gr, g_low = jnp.split(proj, cuts, axis=-1)

    pos = jnp.arange(S)
    aq = rope_partial(aq.reshape(B, S, ATTN_HEADS, HEAD_DIM), pos)
    ak = rope_partial(ak.reshape(B, S, ATTN_HEADS, HEAD_DIM), pos)
    av = av.reshape(B, S, ATTN_HEADS, HEAD_DIM)
    a_out = dilated_attention(aq, ak, av).reshape(B, S, ATTN_WIDTH)
    a_out = rmsnorm(a_out, attn_out_norm)

    z = g_low @ gla_w_gate_up + gla_b_gate
    log_alpha = jax.nn.log_sigmoid(z.astype(F32)) / GLA_TAU
    o = gla_chunked(gq.reshape(B, S, GLA_HEADS, GLA_DK),
                    gk.reshape(B, S, GLA_HEADS, GLA_DK),
                    gv.reshape(B, S, GLA_HEADS, GLA_DV),
                    log_alpha.reshape(B, S, GLA_HEADS, GLA_DK))
    o = rmsnorm(o, gla_out_norm.reshape(GLA_HEADS, GLA_DV)).reshape(B, S, GLA_V_WIDTH)
    g_out = o * jax.nn.silu(gr)

    mixed = jnp.concatenate([a_out, g_out], axis=-1)
    return mixed @ w_out


def swiglu(h, w_gate, w_up, w_down):
    return (jax.nn.silu(h @ w_gate) * (h @ w_up)) @ w_down


def setup_inputs(seed: int = 0) -> dict:
    key = jax.random.key(seed)
    ks = jax.random.split(key, 16)
    L, D = DEPTH, D_MODEL

    def nrm(k, shape, fan_in):
        return jax.random.normal(k, shape, F32) * (fan_in ** -0.5)

    def gain(k, shape):
        return 1.0 + 0.05 * jax.random.normal(k, shape, F32)

    return {
        "x": jax.random.normal(ks[0], (BATCH, SEQ, D), F32),
        "mix_pre_norm": gain(ks[1], (L, D)),
        "mix_post_norm": gain(ks[2], (L, D)),
        "ffn_pre_norm": gain(ks[3], (L, D)),
        "ffn_post_norm": gain(ks[4], (L, D)),
        "w_in": nrm(ks[5], (L, D, IN_WIDTH), D),
        "gla_w_gate_up": nrm(ks[6], (L, GLA_GATE_RANK, GLA_K_WIDTH), GLA_GATE_RANK),
        "gla_b_gate": 0.1 * jax.random.normal(ks[7], (L, GLA_K_WIDTH), F32),
        "gla_out_norm": gain(ks[8], (L, GLA_V_WIDTH)),
        "attn_out_norm": gain(ks[9], (L, ATTN_WIDTH)),
        "w_out": nrm(ks[10], (L, MIX_WIDTH, D), MIX_WIDTH),
        "w_gate": nrm(ks[11], (L, D, D_FF), D),
        "w_up": nrm(ks[12], (L, D, D_FF), D),
        "w_down": nrm(ks[13], (L, D_FF, D), D_FF),
    }


def reference(x, mix_pre_norm, mix_post_norm, ffn_pre_norm, ffn_post_norm, w_in,
              gla_w_gate_up, gla_b_gate, gla_out_norm, attn_out_norm, w_out,
              w_gate, w_up, w_down):
    h = x
    for l in range(DEPTH):
        m = hybrid_mixer(rmsnorm(h, mix_pre_norm[l]), w_in[l], gla_w_gate_up[l], gla_b_gate[l],
                         gla_out_norm[l], attn_out_norm[l], w_out[l])
        h = h + rmsnorm(m, mix_post_norm[l])
        f = swiglu(rmsnorm(h, ffn_pre_norm[l]), w_gate[l], w_up[l], w_down[l])
        h = h + rmsnorm(f, ffn_post_norm[l])
    return h
```

```python
import functools

import numpy as np
import jax
import jax.numpy as jnp
from jax import lax
from jax.experimental import pallas as pl
from jax.experimental.pallas import tpu as pltpu

F32 = jnp.float32
BF16 = jnp.bfloat16

D_MODEL = 1024
HEAD_DIM = 64
ATTN_HEADS = 8
ATTN_WIDTH = ATTN_HEADS * HEAD_DIM
GLA_HEADS = 4
GLA_DK = 64
GLA_DV = 128
GLA_K_WIDTH = GLA_HEADS * GLA_DK
GLA_V_WIDTH = GLA_HEADS * GLA_DV
GLA_GATE_RANK = 16
GLA_TAU = 16.0
MIX_WIDTH = ATTN_WIDTH + GLA_V_WIDTH
D_FF = 2816
ROPE_THETA = 500000.0
ROPE_DIM = HEAD_DIM // 4
ROPE_HALF = ROPE_DIM // 2
DILATIONS = (1, 4, 16)
WINDOW_BACK = 128
RMS_EPS = 1e-6
MAIN_WIDTH = 3 * ATTN_WIDTH + 2 * GLA_K_WIDTH + 2 * GLA_V_WIDTH

LANES = 128
MXU_DIM = 256
VMEM_BYTES_V7X = 64 * 1024 * 1024

ROW_TILE = 512
ATTN_Q_TILE = 512
ATTN_SUB = WINDOW_BACK
HEADS_PER_GROUP = MXU_DIM // HEAD_DIM
GLA_CHUNK = 128
GLA_BLOCK = 512
FF_CHUNK = 256
NEG_BIG = -1e30


def _dot(a, b):
    return jnp.dot(a, b, preferred_element_type=F32)


def _dot_nt(a, b):
    return lax.dot_general(a, b, (((1,), (1,)), ((), ())), preferred_element_type=F32)


def _dot_tn(a, b):
    return lax.dot_general(a, b, (((0,), (0,)), ((), ())), preferred_element_type=F32)


def _rms(x, gain):
    return x * lax.rsqrt(jnp.mean(x * x, axis=-1, keepdims=True) + RMS_EPS) * gain


def _split3(x):
    hi = x.astype(BF16)
    r1 = x - hi.astype(F32)
    mid = r1.astype(BF16)
    lo = (r1 - mid.astype(F32)).astype(BF16)
    return hi, mid, lo


def _vmem_limit(resident_bytes, streamed_bytes):
    need = 2 * (resident_bytes + 2 * streamed_bytes)
    return int(min(max(need, 16 * 1024 * 1024), VMEM_BYTES_V7X - 8 * 1024 * 1024))


def _resident(shape):
    nd = len(shape)
    return pl.BlockSpec(shape, lambda *_: (0,) * nd, pipeline_mode=pl.Buffered(1))


def _inproj_kernel(x_ref, gain_ref, w_ref, wlow_ref, wup_ref, bias_ref, inv_ref,
                   q_ref, k_ref, v_ref, gq_ref, gk_ref, gv_ref, gr_ref, glog_ref, *, tm):
    xb = _rms(x_ref[...], gain_ref[...]).astype(BF16)
    proj = _dot(xb, w_ref[...])

    pos = (pl.program_id(0) * tm + lax.broadcasted_iota(jnp.int32, (tm, LANES), 0)).astype(F32)
    ang = pos * inv_ref[...]
    lane = lax.broadcasted_iota(jnp.int32, (tm, LANES), 1) % HEAD_DIM
    cos, sin = jnp.cos(ang), jnp.sin(ang)
    reps = ATTN_WIDTH // LANES
    cosf = jnp.concatenate([cos] * reps, axis=1)
    sin_first = jnp.concatenate([jnp.where(lane < ROPE_HALF, -sin, 0.0)] * reps, axis=1)
    sin_second = jnp.concatenate([jnp.where(lane >= ROPE_HALF, sin, 0.0)] * reps, axis=1)

    def rope(t):
        return (t * cosf + pltpu.roll(t, ATTN_WIDTH - ROPE_HALF, 1) * sin_first
                + pltpu.roll(t, ROPE_HALF, 1) * sin_second)

    c0 = 0
    q = proj[:, c0:c0 + ATTN_WIDTH]; c0 += ATTN_WIDTH
    k = proj[:, c0:c0 + ATTN_WIDTH]; c0 += ATTN_WIDTH
    v = proj[:, c0:c0 + ATTN_WIDTH]; c0 += ATTN_WIDTH
    gq = proj[:, c0:c0 + GLA_K_WIDTH]; c0 += GLA_K_WIDTH
    gk = proj[:, c0:c0 + GLA_K_WIDTH]; c0 += GLA_K_WIDTH
    gv = proj[:, c0:c0 + GLA_V_WIDTH]; c0 += GLA_V_WIDTH
    gr = proj[:, c0:c0 + GLA_V_WIDTH]
    q_ref[...] = (rope(q) * (HEAD_DIM ** -0.5)).astype(BF16)
    k_ref[...] = rope(k).astype(BF16)
    v_ref[...] = v.astype(BF16)
    gq_ref[...] = (gq * (GLA_DK ** -0.5)).astype(BF16)
    gk_ref[...] = gk.astype(BF16)
    gv_ref[...] = gv.astype(BF16)
    gr_ref[...] = gr.astype(BF16)

    g_low = _dot(xb, wlow_ref[...])
    z = _dot(g_low.astype(BF16), wup_ref[...]) + bias_ref[...]
    log_sig = jnp.minimum(z, 0.0) - jnp.log1p(jnp.exp(-jnp.abs(z)))
    glog_ref[...] = log_sig * (1.0 / GLA_TAU)


def _inproj(h, gain, w_main, w_low, w_up, bias, inv_freq):
    S = h.shape[0]
    tm = ROW_TILE
    row = lambda width: pl.BlockSpec((tm, width), lambda i: (i, 0))
    widths = (ATTN_WIDTH,) * 3 + (GLA_K_WIDTH,) * 2 + (GLA_V_WIDTH,) * 2
    out_shape = [jax.ShapeDtypeStruct((S, w), BF16) for w in widths] + [jax.ShapeDtypeStruct((S, GLA_K_WIDTH), F32)]
    resident = 2 * (w_main.size + w_low.size + w_up.size)
    streamed = tm * (D_MODEL * 4 + MAIN_WIDTH * 2 + GLA_K_WIDTH * 4) + tm * MAIN_WIDTH * 4
    return pl.pallas_call(
        functools.partial(_inproj_kernel, tm=tm),
        grid=(S // tm,),
        in_specs=[row(D_MODEL), _resident((1, D_MODEL)), _resident(w_main.shape), _resident(w_low.shape),
                  _resident(w_up.shape), _resident((1, GLA_K_WIDTH)), _resident((1, LANES))],
        out_specs=[row(w) for w in widths] + [row(GLA_K_WIDTH)],
        out_shape=out_shape,
        compiler_params=pltpu.CompilerParams(dimension_semantics=("arbitrary",),
                                             vmem_limit_bytes=_vmem_limit(resident, streamed)),
        name="inproj",
    )(h, gain, w_main, w_low, w_up, bias, inv_freq)


def _dilattn_kernel(q_ref, kc_ref, kp_ref, vc_ref, vp_ref, o_ref, lse_ref, *, tq):
    first_tile = pl.program_id(1) == 0
    sub = ATTN_SUB
    grp = MXU_DIM
    a = lax.broadcasted_iota(jnp.int32, (sub, 2 * sub), 0)
    c = lax.broadcasted_iota(jnp.int32, (sub, 2 * sub), 1)
    band = (c >= a) & (c <= a + WINDOW_BACK)
    band_first = band & ((c >= sub) | jnp.logical_not(first_tile))
    lane_head = lax.broadcasted_iota(jnp.int32, (sub, grp), 1) // HEAD_DIM
    head_lane = lax.broadcasted_iota(jnp.int32, (sub, LANES), 1)
    ones = jnp.ones((2 * sub, LANES), BF16)

    for j in range(tq // sub):
        rows = slice(j * sub, (j + 1) * sub)
        valid = jnp.concatenate([band_first if j == 0 else band] * HEADS_PER_GROUP, axis=0)
        lse_tile = jnp.zeros((sub, LANES), F32)
        for g in range(ATTN_WIDTH // grp):
            cols = slice(g * grp, (g + 1) * grp)
            qg = q_ref[rows, cols]
            if j == 0:
                kg = jnp.concatenate([kp_ref[:, cols], kc_ref[0:sub, cols]], axis=0)
                vg = jnp.concatenate([vp_ref[:, cols], vc_ref[0:sub, cols]], axis=0)
            else:
                kg = kc_ref[(j - 1) * sub:(j + 1) * sub, cols]
                vg = vc_ref[(j - 1) * sub:(j + 1) * sub, cols]
            qs = jnp.concatenate([jnp.where(lane_head == hh, qg, jnp.zeros_like(qg))
                                  for hh in range(HEADS_PER_GROUP)], axis=0)
            s = jnp.where(valid, _dot_nt(qs, kg), NEG_BIG)
            m = jnp.max(s, axis=-1, keepdims=True)
            p = jnp.exp(s - m).astype(BF16)
            denom = _dot(p, ones)
            pv = _dot(p, vg)
            o_grp = jnp.zeros((sub, grp), F32)
            for hh in range(HEADS_PER_GROUP):
                hr = slice(hh * sub, (hh + 1) * sub)
                inv = 1.0 / denom[hr]
                o_h = pv[hr] * jnp.concatenate([inv] * (grp // LANES), axis=1)
                o_grp = jnp.where(lane_head == hh, o_h, o_grp)
                lse_h = m[hr] + jnp.log(denom[hr])
                lse_tile = jnp.where(head_lane == g * HEADS_PER_GROUP + hh, lse_h, lse_tile)
            o_ref[rows, cols] = o_grp
        lse_ref[rows, :] = lse_tile


def _dilattn(q, k, v, d):
    S = q.shape[0]
    rows = S // d
    tq = min(ATTN_Q_TILE, rows)
    W = ATTN_WIDTH
    view = lambda t: t.reshape(rows, d * W)
    cur = pl.BlockSpec((tq, W), lambda r, i: (i, r))
    prev = pl.BlockSpec((ATTN_SUB, W), lambda r, i: (jnp.maximum(i * (tq // ATTN_SUB) - 1, 0), r))
    streamed = tq * W * (3 * 2 + 4) + 2 * ATTN_SUB * W * 2 + tq * LANES * 4
    o, lse = pl.pallas_call(
        functools.partial(_dilattn_kernel, tq=tq),
        grid=(d, rows // tq),
        in_specs=[cur, cur, prev, cur, prev],
        out_specs=[pl.BlockSpec((tq, W), lambda r, i: (i, r)), pl.BlockSpec((tq, LANES), lambda r, i: (i, r))],
        out_shape=[jax.ShapeDtypeStruct((rows, d * W), F32), jax.ShapeDtypeStruct((rows, d * LANES), F32)],
        compiler_params=pltpu.CompilerParams(dimension_semantics=("arbitrary", "arbitrary"),
                                             vmem_limit_bytes=_vmem_limit(0, streamed)),
        name=f"dilattn{d}",
    )(view(q), view(k), view(k), view(v), view(v))
    return o.reshape(S, W), lse.reshape(S, LANES)


def _attnmerge_kernel(o1_ref, o2_ref, o3_ref, l1_ref, l2_ref, l3_ref, expand_ref, gain_ref, out_ref):
    lses = (l1_ref[...], l2_ref[...], l3_ref[...])
    top = jnp.maximum(jnp.maximum(lses[0], lses[1]), lses[2])
    ws = [jnp.exp(l - top) for l in lses]
    inv_total = 1.0 / (ws[0] + ws[1] + ws[2])
    acc = None
    for w, o_ref in zip(ws, (o1_ref, o2_ref, o3_ref)):
        parts = _split3(w * inv_total)
        w_wide = sum(_dot(p, expand_ref[...]) for p in parts)
        term = w_wide * o_ref[...]
        acc = term if acc is None else acc + term
    out_ref[...] = _rms(acc, gain_ref[...]).astype(BF16)


def _attnmerge(os_, lses, gain):
    S = os_[0].shape[0]
    tm = ROW_TILE
    expand = np.zeros((LANES, ATTN_WIDTH), np.float32)
    for h in range(ATTN_HEADS):
        expand[h, h * HEAD_DIM:(h + 1) * HEAD_DIM] = 1.0
    row = lambda width: pl.BlockSpec((tm, width), lambda i: (i, 0))
    streamed = tm * (3 * ATTN_WIDTH * 4 + 3 * LANES * 4 + ATTN_WIDTH * 2)
    return pl.pallas_call(
        _attnmerge_kernel,
        grid=(S // tm,),
        in_specs=[row(ATTN_WIDTH)] * 3 + [row(LANES)] * 3 + [_resident((LANES, ATTN_WIDTH)), _resident((1, ATTN_WIDTH))],
        out_specs=row(ATTN_WIDTH),
        out_shape=jax.ShapeDtypeStruct((S, ATTN_WIDTH), BF16),
        compiler_params=pltpu.CompilerParams(dimension_semantics=("arbitrary",),
                                             vmem_limit_bytes=_vmem_limit(0, streamed)),
        name="attnmerge",
    )(*os_, *lses, jnp.asarray(expand, BF16), gain)


def _gla_tables(C):
    levels = int(np.log2(C))
    t = np.arange(C)[:, None]
    j = np.arange(C)[None, :]
    blocks = [(j <= t), (j > t)]
    masks = []
    for l in range(levels):
        c = 1 << l
        m = (t // (2 * c)) * (2 * c) + c - 1
        second = (t % (2 * c)) >= c
        blocks.append(np.where(second, (j > m) & (j <= t), (j > t) & (j <= m)))
        masks.append((t // (2 * c) == j // (2 * c)) & second & ((j % (2 * c)) < c))
    masks.append(t == j)
    seg = np.concatenate(blocks, axis=0).astype(np.float32)
    msk = np.stack([np.tile(mm, (GLA_HEADS, 1)) for mm in masks]).astype(np.float32)
    return seg, msk


def _gla_kernel(q_ref, k_ref, v_ref, g_ref, r_ref, gain_ref, seg_ref, msk_ref, o_ref, state_ref, *, tb):
    C = GLA_CHUNK
    levels = msk_ref.shape[0] - 1
    lane_head = lax.broadcasted_iota(jnp.int32, (C, GLA_K_WIDTH), 1) // GLA_DK

    @pl.when(pl.program_id(0) == 0)
    def _():
        state_ref[...] = jnp.zeros_like(state_ref)

    def stack_heads(t):
        return jnp.concatenate([jnp.where(lane_head == hh, t, jnp.zeros_like(t)) for hh in range(GLA_HEADS)], axis=0)

    def chunk(ci, carry):
        rows = pl.ds(pl.multiple_of(ci * C, C), C)
        q = q_ref[rows, :].astype(F32)
        k = k_ref[rows, :].astype(F32)
        v = v_ref[rows, :]
        seg = seg_ref[...]
        expo = sum(_dot(seg, part) for part in _split3(g_ref[rows, :]))
        u = jnp.exp(jnp.minimum(expo, 0.0))
        u_from_start, u_to_end = u[0:C], u[C:2 * C]

        scores = msk_ref[levels] * _dot_nt(stack_heads(q).astype(BF16), k.astype(BF16))
        for l in range(levels):
            ul = u[(2 + l) * C:(3 + l) * C]
            scores = scores + msk_ref[l] * _dot_nt(stack_heads(q * ul).astype(BF16), (k * ul).astype(BF16))

        state_t = state_ref[...]
        inter = _dot_nt(stack_heads(q * u_from_start).astype(BF16), state_t.astype(BF16))
        scores = scores.astype(BF16)
        gain = gain_ref[...]
        for hh in range(GLA_HEADS):
            hr = slice(hh * C, (hh + 1) * C)
            vc = slice(hh * GLA_DV, (hh + 1) * GLA_DV)
            o_h = _dot(scores[hr], v[:, vc]) + inter[hr]
            o_h = _rms(o_h, gain[:, vc])
            gate = r_ref[rows, vc].astype(F32)
            o_ref[rows, vc] = (o_h * (gate * jax.nn.sigmoid(gate))).astype(BF16)

        upd = _dot_tn(v, (k * u_to_end).astype(BF16))
        new_state = state_t * u_from_start[C - 1:C, :]
        for hh in range(GLA_HEADS):
            new_state = new_state + jnp.where(lane_head == hh, upd[hh * GLA_DV:(hh + 1) * GLA_DV], 0.0)
        state_ref[...] = new_state
        return carry

    lax.fori_loop(0, tb // C, chunk, 0)


def _gla(gq, gk, gv, glog, gr, gain):
    S = gq.shape[0]
    tb = GLA_BLOCK
    seg, msk = _gla_tables(GLA_CHUNK)
    row = lambda width: pl.BlockSpec((tb, width), lambda i: (i, 0))
    resident = seg.size * 2 + msk.size * 4
    streamed = tb * (2 * GLA_K_WIDTH * 2 + 2 * GLA_V_WIDTH * 2 + GLA_K_WIDTH * 4 + GLA_V_WIDTH * 2)
    return pl.pallas_call(
        functools.partial(_gla_kernel, tb=tb),
        grid=(S // tb,),
        in_specs=[row(GLA_K_WIDTH), row(GLA_K_WIDTH), row(GLA_V_WIDTH), row(GLA_K_WIDTH), row(GLA_V_WIDTH),
                  _resident((1, GLA_V_WIDTH)), _resident(seg.shape), _resident(msk.shape)],
        out_specs=row(GLA_V_WIDTH),
        out_shape=jax.ShapeDtypeStruct((S, GLA_V_WIDTH), BF16),
        scratch_shapes=[pltpu.VMEM((GLA_DV, GLA_K_WIDTH), F32)],
        compiler_params=pltpu.CompilerParams(dimension_semantics=("arbitrary",),
                                             vmem_limit_bytes=_vmem_limit(resident, streamed)),
        name="gla",
    )(gq, gk, gv, glog, gr, gain, jnp.asarray(seg, BF16), jnp.asarray(msk, F32))


def _outproj_kernel(a_ref, g_ref, h_ref, wa_ref, wg_ref, post_ref, pre_ref, hout_ref, fin_ref):
    m = _dot(a_ref[...], wa_ref[...]) + _dot(g_ref[...], wg_ref[...])
    h = h_ref[...] + _rms(m, post_ref[...])
    hout_ref[...] = h
    fin_ref[...] = _rms(h, pre_ref[...]).astype(BF16)


def _outproj(a_out, g_out, h, w_a, w_g, post_gain, pre_gain):
    S = h.shape[0]
    tm = ROW_TILE
    row = lambda width: pl.BlockSpec((tm, width), lambda i: (i, 0))
    resident = 2 * (w_a.size + w_g.size)
    streamed = tm * (MIX_WIDTH * 2 + D_MODEL * (4 + 4 + 2))
    return pl.pallas_call(
        _outproj_kernel,
        grid=(S // tm,),
        in_specs=[row(ATTN_WIDTH), row(GLA_V_WIDTH), row(D_MODEL), _resident(w_a.shape), _resident(w_g.shape),
                  _resident((1, D_MODEL)), _resident((1, D_MODEL))],
        out_specs=[row(D_MODEL), row(D_MODEL)],
        out_shape=[jax.ShapeDtypeStruct((S, D_MODEL), F32), jax.ShapeDtypeStruct((S, D_MODEL), BF16)],
        compiler_params=pltpu.CompilerParams(dimension_semantics=("arbitrary",),
                                             vmem_limit_bytes=_vmem_limit(resident, streamed)),
        name="outproj",
    )(a_out, g_out, h, w_a, w_g, post_gain, pre_gain)


def _ffn_kernel(x_ref, h_ref, wg_ref, wu_ref, wd_ref, post_ref, out_ref):
    x = x_ref[...]
    acc = None
    for c0 in range(0, D_FF, FF_CHUNK):
        cols = slice(c0, c0 + FF_CHUNK)
        gate = _dot(x, wg_ref[:, cols])
        act = (gate * jax.nn.sigmoid(gate) * _dot(x, wu_ref[:, cols])).astype(BF16)
        part = _dot(act, wd_ref[cols, :])
        acc = part if acc is None else acc + part
    out_ref[...] = h_ref[...] + _rms(acc, post_ref[...])


def _ffn(f_in, h, w_gate, w_up, w_down, post_gain):
    S = h.shape[0]
    tm = ROW_TILE
    row = lambda width: pl.BlockSpec((tm, width), lambda i: (i, 0))
    resident = 2 * (w_gate.size + w_up.size + w_down.size)
    streamed = tm * D_MODEL * (2 + 4 + 4)
    return pl.pallas_call(
        _ffn_kernel,
        grid=(S // tm,),
        in_specs=[row(D_MODEL), row(D_MODEL), _resident(w_gate.shape), _resident(w_up.shape), _resident(w_down.shape),
                  _resident((1, D_MODEL))],
        out_specs=row(D_MODEL),
        out_shape=jax.ShapeDtypeStruct((S, D_MODEL), F32),
        compiler_params=pltpu.CompilerParams(dimension_semantics=("arbitrary",),
                                             vmem_limit_bytes=_vmem_limit(resident, streamed)),
        name="ffn",
    )(f_in, h, w_gate, w_up, w_down, post_gain)


def _rope_inv_freq():
    inv = np.zeros((1, LANES), np.float32)
    freqs = np.asarray(ROPE_THETA, np.float32) ** (-(np.arange(ROPE_HALF, dtype=np.float32) * 2.0) / ROPE_DIM)
    for lane in range(LANES):
        if lane % HEAD_DIM < ROPE_DIM:
            inv[0, lane] = freqs[lane % ROPE_HALF]
    return jnp.asarray(inv)


def _layer(h, mix_pre, mix_post, ffn_pre, ffn_post, w_in, w_gate_up, b_gate, gla_norm, attn_norm, w_out,
           w_gate, w_up, w_down, inv_freq):
    row = lambda t: t.reshape(1, -1)
    w_main = w_in[:, :MAIN_WIDTH].astype(BF16)
    w_low = jnp.pad(w_in[:, MAIN_WIDTH:], ((0, 0), (0, LANES - GLA_GATE_RANK))).astype(BF16)
    w_gu = jnp.pad(w_gate_up, ((0, LANES - GLA_GATE_RANK), (0, 0))).astype(BF16)
    q, k, v, gq, gk, gv, gr, glog = _inproj(h, row(mix_pre), w_main, w_low, w_gu, row(b_gate), inv_freq)

    pats = [_dilattn(q, k, v, d) for d in DILATIONS]
    a_out = _attnmerge([p[0] for p in pats], [p[1] for p in pats], row(attn_norm))
    g_out = _gla(gq, gk, gv, glog, gr, row(gla_norm))

    w_o = w_out.astype(BF16)
    h, f_in = _outproj(a_out, g_out, h, w_o[:ATTN_WIDTH], w_o[ATTN_WIDTH:], row(mix_post), row(ffn_pre))
    return _ffn(f_in, h, w_gate.astype(BF16), w_up.astype(BF16), w_down.astype(BF16), row(ffn_post))


def kernel(x, mix_pre_norm, mix_post_norm, ffn_pre_norm, ffn_post_norm, w_in, gla_w_gate_up, gla_b_gate,
           gla_out_norm, attn_out_norm, w_out, w_gate, w_up, w_down):
    B, S, D = x.shape
    assert D == D_MODEL and S % (max(DILATIONS) * ATTN_SUB) == 0 and S % GLA_BLOCK == 0 and S % ROW_TILE == 0
    inv_freq = _rope_inv_freq()
    outs = []
    for b in range(B):
        h = x[b]
        for l in range(mix_pre_norm.shape[0]):
            h = _layer(h, mix_pre_norm[l], mix_post_norm[l], ffn_pre_norm[l], ffn_post_norm[l], w_in[l],
                       gla_w_gate_up[l], gla_b_gate[l], gla_out_norm[l], attn_out_norm[l], w_out[l],
                       w_gate[l], w_up[l], w_down[l], inv_freq)
        outs.append(h)
    return jnp.stack(outs, axis=0)
```

```python
import functools

import numpy as np
import jax
import jax.numpy as jnp
from jax import lax
from jax.experimental import pallas as pl
from jax.experimental.pallas import tpu as pltpu

F32 = jnp.float32
BF16 = jnp.bfloat16

D_MODEL = 1024
HEAD_DIM = 64
ATTN_HEADS = 8
ATTN_WIDTH = ATTN_HEADS * HEAD_DIM
GLA_HEADS = 4
GLA_DK = 64
GLA_DV = 128
GLA_K_WIDTH = GLA_HEADS * GLA_DK
GLA_V_WIDTH = GLA_HEADS * GLA_DV
GLA_GATE_RANK = 16
GLA_TAU = 16.0
MIX_WIDTH = ATTN_WIDTH + GLA_V_WIDTH
D_FF = 2816
ROPE_THETA = 500000.0
ROPE_DIM = HEAD_DIM // 4
ROPE_HALF = ROPE_DIM // 2
DILATIONS = (1, 4, 16)
WINDOW_BACK = 128
RMS_EPS = 1e-6
MAIN_WIDTH = 3 * ATTN_WIDTH + 2 * GLA_K_WIDTH + 2 * GLA_V_WIDTH

LANES = 128
MXU_DIM = 256
VMEM_BYTES_V7X = 64 * 1024 * 1024

ROW_TILE = 512
ATTN_Q_TILE = 512
ATTN_SUB = WINDOW_BACK
HEADS_PER_GROUP = MXU_DIM // HEAD_DIM
ATTN_GROUPS = ATTN_WIDTH // MXU_DIM
ATTN_SLABS = ATTN_WIDTH // LANES
STAT_SUM_LANE = ATTN_HEADS
GLA_CHUNK = 128
GLA_BLOCK = 512
FF_CHUNK = 256
NEG_BIG = -1e30


def _dot(a, b):
    return jnp.dot(a, b, preferred_element_type=F32)


def _dot_nt(a, b):
    return lax.dot_general(a, b, (((1,), (1,)), ((), ())), preferred_element_type=F32)


def _dot_tn(a, b):
    return lax.dot_general(a, b, (((0,), (0,)), ((), ())), preferred_element_type=F32)


def _rms(x, gain):
    return x * lax.rsqrt(jnp.mean(x * x, axis=-1, keepdims=True) + RMS_EPS) * gain


def _split3(x):
    hi = x.astype(BF16)
    r1 = x - hi.astype(F32)
    mid = r1.astype(BF16)
    lo = (r1 - mid.astype(F32)).astype(BF16)
    return hi, mid, lo


def _vmem_limit(resident_bytes, streamed_bytes):
    need = 2 * (resident_bytes + 2 * streamed_bytes)
    return int(min(max(need, 16 * 1024 * 1024), VMEM_BYTES_V7X - 8 * 1024 * 1024))


def _resident(shape):
    nd = len(shape)
    return pl.BlockSpec(shape, lambda *_: (0,) * nd, pipeline_mode=pl.Buffered(1))


def _layer_block(layer, shape, index=None):
    index = (0,) * len(shape) if index is None else index
    return pl.BlockSpec((None,) + tuple(shape), lambda *_: (layer,) + tuple(index), pipeline_mode=pl.Buffered(1))


def _rows(tm, width):
    return pl.BlockSpec((tm, width), lambda i: (i, 0))


def _inproj_kernel(x_ref, gain_ref, w_ref, wlow_ref, wup_ref, bias_ref, inv_ref, *refs, tm):
    qkv_refs = refs[:3 * len(DILATIONS)]
    gq_ref, gk_ref, gv_ref, gr_ref, glog_ref, slab_ref = refs[3 * len(DILATIONS):]
    xb = _rms(x_ref[...], gain_ref[...]).astype(BF16)
    proj = _dot(xb, w_ref[...])

    pos = (pl.program_id(0) * tm + lax.broadcasted_iota(jnp.int32, (tm, LANES), 0)).astype(F32)
    ang = pos * inv_ref[...]
    lane = lax.broadcasted_iota(jnp.int32, (tm, LANES), 1) % HEAD_DIM
    cos, sin = jnp.cos(ang), jnp.sin(ang)
    cosf = jnp.concatenate([cos] * ATTN_SLABS, axis=1)
    sin_first = jnp.concatenate([jnp.where(lane < ROPE_HALF, -sin, 0.0)] * ATTN_SLABS, axis=1)
    sin_second = jnp.concatenate([jnp.where(lane >= ROPE_HALF, sin, 0.0)] * ATTN_SLABS, axis=1)

    def rope(t):
        return (t * cosf + pltpu.roll(t, ATTN_WIDTH - ROPE_HALF, 1) * sin_first
                + pltpu.roll(t, ROPE_HALF, 1) * sin_second)

    def emit_layouts(t, out_refs):
        for s in range(ATTN_SLABS):
            slab_ref[s] = t[:, s * LANES:(s + 1) * LANES]
        for d, out_ref in zip(DILATIONS, out_refs):
            if d == 1:
                out_ref[...] = t.astype(BF16)
                continue
            n = tm // d
            for r in range(d):
                part = jnp.concatenate([slab_ref[s, pl.ds(r, n, stride=d), :] for s in range(ATTN_SLABS)], axis=1)
                out_ref[:, r * ATTN_WIDTH:(r + 1) * ATTN_WIDTH] = part.astype(BF16)

    n_d = len(DILATIONS)
    c0 = 0
    q = proj[:, c0:c0 + ATTN_WIDTH]; c0 += ATTN_WIDTH
    k = proj[:, c0:c0 + ATTN_WIDTH]; c0 += ATTN_WIDTH
    v = proj[:, c0:c0 + ATTN_WIDTH]; c0 += ATTN_WIDTH
    gq = proj[:, c0:c0 + GLA_K_WIDTH]; c0 += GLA_K_WIDTH
    gk = proj[:, c0:c0 + GLA_K_WIDTH]; c0 += GLA_K_WIDTH
    gv = proj[:, c0:c0 + GLA_V_WIDTH]; c0 += GLA_V_WIDTH
    gr = proj[:, c0:c0 + GLA_V_WIDTH]
    emit_layouts(rope(q) * (HEAD_DIM ** -0.5), qkv_refs[0:n_d])
    emit_layouts(rope(k), qkv_refs[n_d:2 * n_d])
    emit_layouts(v, qkv_refs[2 * n_d:3 * n_d])
    gq_ref[...] = (gq * (GLA_DK ** -0.5)).astype(BF16)
    gk_ref[...] = gk.astype(BF16)
    gv_ref[...] = gv.astype(BF16)
    gr_ref[...] = gr.astype(BF16)

    g_low = _dot(xb, wlow_ref[...])
    z = _dot(g_low.astype(BF16), wup_ref[...]) + bias_ref[...]
    log_sig = jnp.minimum(z, 0.0) - jnp.log1p(jnp.exp(-jnp.abs(z)))
    glog_ref[...] = log_sig * (1.0 / GLA_TAU)


def _inproj(h, layer, gain, w_in, w_low, w_up, bias, inv_freq):
    S = h.shape[0]
    tm = ROW_TILE
    qkv_specs, qkv_shapes = [], []
    for _ in range(3):
        for d in DILATIONS:
            qkv_specs.append(_rows(tm // d, d * ATTN_WIDTH))
            qkv_shapes.append(jax.ShapeDtypeStruct((S // d, d * ATTN_WIDTH), BF16))
    gla_widths = (GLA_K_WIDTH,) * 2 + (GLA_V_WIDTH,) * 2
    out_shape = qkv_shapes + [jax.ShapeDtypeStruct((S, w), BF16) for w in gla_widths] + [jax.ShapeDtypeStruct((S, GLA_K_WIDTH), F32)]
    resident = 2 * (D_MODEL * MAIN_WIDTH + 2 * D_MODEL * LANES)
    streamed = tm * (D_MODEL * 4 + 3 * len(DILATIONS) * ATTN_WIDTH * 2 + MAIN_WIDTH * 2) + tm * MAIN_WIDTH * 4
    return pl.pallas_call(
        functools.partial(_inproj_kernel, tm=tm),
        grid=(S // tm,),
        in_specs=[_rows(tm, D_MODEL), _layer_block(layer, (1, D_MODEL)), _layer_block(layer, (D_MODEL, MAIN_WIDTH)),
                  _layer_block(layer, (D_MODEL, LANES)), _layer_block(layer, (LANES, GLA_K_WIDTH)),
                  _layer_block(layer, (1, GLA_K_WIDTH)), _resident((1, LANES))],
        out_specs=qkv_specs + [_rows(tm, w) for w in gla_widths] + [_rows(tm, GLA_K_WIDTH)],
        out_shape=out_shape,
        scratch_shapes=[pltpu.VMEM((ATTN_SLABS, tm, LANES), F32)],
        compiler_params=pltpu.CompilerParams(dimension_semantics=("arbitrary",),
                                             vmem_limit_bytes=_vmem_limit(resident, streamed)),
        name="inproj",
    )(h, gain, w_in, w_low, w_up, bias, inv_freq)


def _dilattn_kernel(q_ref, kc_ref, kp_ref, vc_ref, vp_ref, o_ref, stat_ref, *, tq):
    first_tile = pl.program_id(1) == 0
    sub = ATTN_SUB
    grp = MXU_DIM
    a = lax.broadcasted_iota(jnp.int32, (sub, 2 * sub), 0)
    c = lax.broadcasted_iota(jnp.int32, (sub, 2 * sub), 1)
    band = (c >= a) & (c <= a + WINDOW_BACK)
    band_first = band & ((c >= sub) | jnp.logical_not(first_tile))
    q_head = lax.broadcasted_iota(jnp.int32, (sub, grp), 1) // HEAD_DIM
    stat_lane = lax.broadcasted_iota(jnp.int32, (sub, LANES), 1)
    ones = jnp.ones((2 * sub, LANES), BF16)

    for j in range(tq // sub):
        rows = slice(j * sub, (j + 1) * sub)
        valid = jnp.concatenate([band_first if j == 0 else band] * HEADS_PER_GROUP, axis=0)
        stats = jnp.zeros((sub, LANES), F32)
        for g in range(ATTN_GROUPS):
            cols = slice(g * grp, (g + 1) * grp)
            qg = q_ref[rows, cols]
            if j == 0:
                kg = jnp.concatenate([kp_ref[:, cols], kc_ref[0:sub, cols]], axis=0)
                vg = jnp.concatenate([vp_ref[:, cols], vc_ref[0:sub, cols]], axis=0)
            else:
                kg = kc_ref[(j - 1) * sub:(j + 1) * sub, cols]
                vg = vc_ref[(j - 1) * sub:(j + 1) * sub, cols]
            qs = jnp.concatenate([jnp.where(q_head == hh, qg, jnp.zeros_like(qg))
                                  for hh in range(HEADS_PER_GROUP)], axis=0)
            s = jnp.where(valid, _dot_nt(qs, kg), NEG_BIG)
            m = jnp.max(s, axis=-1, keepdims=True)
            p = jnp.exp(s - m).astype(BF16)
            sums = _dot(p, ones)
            pv = _dot(p, vg)
            o_grp = jnp.zeros((sub, grp), F32)
            for hh in range(HEADS_PER_GROUP):
                hr = slice(hh * sub, (hh + 1) * sub)
                head = g * HEADS_PER_GROUP + hh
                o_grp = jnp.where(q_head == hh, pv[hr], o_grp)
                stats = jnp.where(stat_lane == head, m[hr], stats)
                stats = jnp.where(stat_lane == STAT_SUM_LANE + head, sums[hr], stats)
            o_ref[rows, cols] = o_grp
        stat_ref[rows, :] = stats


def _dilattn(q, k, v, d):
    rows = q.shape[0]
    tq = min(ATTN_Q_TILE, rows)
    W = ATTN_WIDTH
    cur = pl.BlockSpec((tq, W), lambda r, i: (i, r))
    prev = pl.BlockSpec((ATTN_SUB, W), lambda r, i: (jnp.maximum(i * (tq // ATTN_SUB) - 1, 0), r))
    streamed = tq * W * (3 * 2 + 4) + 2 * ATTN_SUB * W * 2 + tq * LANES * 4
    return pl.pallas_call(
        functools.partial(_dilattn_kernel, tq=tq),
        grid=(d, rows // tq),
        in_specs=[cur, cur, prev, cur, prev],
        out_specs=[pl.BlockSpec((tq, W), lambda r, i: (i, r)), pl.BlockSpec((tq, LANES), lambda r, i: (i, r))],
        out_shape=[jax.ShapeDtypeStruct((rows, d * W), F32), jax.ShapeDtypeStruct((rows, d * LANES), F32)],
        compiler_params=pltpu.CompilerParams(dimension_semantics=("arbitrary", "arbitrary"),
                                             vmem_limit_bytes=_vmem_limit(0, streamed)),
        name=f"dilattn{d}",
    )(q, k, k, v, v)


def _attnmerge_kernel(*refs, tm):
    n_d = len(DILATIONS)
    o_refs, st_refs = refs[:n_d], refs[n_d:2 * n_d]
    expand_ref, gain_ref, out_ref, oslab_ref, stslab_ref = refs[2 * n_d:]

    def natural_rows(ref, d, width, slab_ref):
        if d == 1:
            return ref[...]
        n = tm // d
        slabs = width // LANES
        for r in range(d):
            for s in range(slabs):
                slab_ref[s, pl.ds(r, n, stride=d), :] = ref[:, r * width + s * LANES:r * width + (s + 1) * LANES]
        return jnp.concatenate([slab_ref[s] for s in range(slabs)], axis=1)

    stats = [natural_rows(st_refs[i], d, LANES, stslab_ref.at[i]) for i, d in enumerate(DILATIONS)]
    lane = lax.broadcasted_iota(jnp.int32, (tm, LANES), 1)
    top = functools.reduce(jnp.maximum, stats)
    scale = [jnp.exp(st - top) for st in stats]
    total = sum(sc * pltpu.roll(st, LANES - STAT_SUM_LANE, 1) for sc, st in zip(scale, stats))
    inv_total = jnp.where(lane < ATTN_HEADS, 1.0 / total, 0.0)
    acc = None
    for i, d in enumerate(DILATIONS):
        parts = _split3(scale[i] * inv_total)
        coef = sum(_dot(p, expand_ref[...]) for p in parts)
        term = coef * natural_rows(o_refs[i], d, ATTN_WIDTH, oslab_ref)
        acc = term if acc is None else acc + term
    out_ref[...] = _rms(acc, gain_ref[...]).astype(BF16)


def _attnmerge(os_, stats, layer, gain):
    S = os_[0].shape[0]
    tm = ROW_TILE
    expand = np.zeros((LANES, ATTN_WIDTH), np.float32)
    for h in range(ATTN_HEADS):
        expand[h, h * HEAD_DIM:(h + 1) * HEAD_DIM] = 1.0
    streamed = tm * (3 * ATTN_WIDTH * 4 + 3 * LANES * 4 + ATTN_WIDTH * 2)
    scratch_bytes = (ATTN_SLABS + len(DILATIONS)) * tm * LANES * 4
    return pl.pallas_call(
        functools.partial(_attnmerge_kernel, tm=tm),
        grid=(S // tm,),
        in_specs=([_rows(tm // d, d * ATTN_WIDTH) for d in DILATIONS] + [_rows(tm // d, d * LANES) for d in DILATIONS]
                  + [_resident((LANES, ATTN_WIDTH)), _layer_block(layer, (1, ATTN_WIDTH))]),
        out_specs=_rows(tm, ATTN_WIDTH),
        out_shape=jax.ShapeDtypeStruct((S, ATTN_WIDTH), BF16),
        scratch_shapes=[pltpu.VMEM((ATTN_SLABS, tm, LANES), F32), pltpu.VMEM((len(DILATIONS), 1, tm, LANES), F32)],
        compiler_params=pltpu.CompilerParams(dimension_semantics=("arbitrary",),
                                             vmem_limit_bytes=_vmem_limit(scratch_bytes, streamed)),
        name="attnmerge",
    )(*os_, *stats, jnp.asarray(expand, BF16), gain)


def _gla_tables(C):
    levels = int(np.log2(C))
    t = np.arange(C)[:, None]
    j = np.arange(C)[None, :]
    blocks = [(j <= t), (j > t)]
    masks = []
    for l in range(levels):
        c = 1 << l
        m = (t // (2 * c)) * (2 * c) + c - 1
        second = (t % (2 * c)) >= c
        blocks.append(np.where(second, (j > m) & (j <= t), (j > t) & (j <= m)))
        masks.append((t // (2 * c) == j // (2 * c)) & second & ((j % (2 * c)) < c))
    masks.append(t == j)
    seg = np.concatenate(blocks, axis=0).astype(np.float32)
    msk = np.stack([np.tile(mm, (GLA_HEADS, 1)) for mm in masks]).astype(np.float32)
    return seg, msk


def _gla_kernel(q_ref, k_ref, v_ref, g_ref, r_ref, gain_ref, seg_ref, msk_ref, o_ref, state_ref, *, tb):
    C = GLA_CHUNK
    levels = msk_ref.shape[0] - 1
    lane_head = lax.broadcasted_iota(jnp.int32, (C, GLA_K_WIDTH), 1) // GLA_DK

    @pl.when(pl.program_id(0) == 0)
    def _():
        state_ref[...] = jnp.zeros_like(state_ref)

    def stack_heads(t):
        return jnp.concatenate([jnp.where(lane_head == hh, t, jnp.zeros_like(t)) for hh in range(GLA_HEADS)], axis=0)

    def chunk(ci, carry):
        rows = pl.ds(pl.multiple_of(ci * C, C), C)
        q = q_ref[rows, :].astype(F32)
        k = k_ref[rows, :].astype(F32)
        v = v_ref[rows, :]
        seg = seg_ref[...]
        expo = sum(_dot(seg, part) for part in _split3(g_ref[rows, :]))
        u = jnp.exp(jnp.minimum(expo, 0.0))
        u_from_start, u_to_end = u[0:C], u[C:2 * C]

        scores = msk_ref[levels] * _dot_nt(stack_heads(q).astype(BF16), k.astype(BF16))
        for l in range(levels):
            ul = u[(2 + l) * C:(3 + l) * C]
            scores = scores + msk_ref[l] * _dot_nt(stack_heads(q * ul).astype(BF16), (k * ul).astype(BF16))

        state_t = state_ref[...]
        inter = _dot_nt(stack_heads(q * u_from_start).astype(BF16), state_t.astype(BF16))
        scores = scores.astype(BF16)
        gain = gain_ref[...]
        for hh in range(GLA_HEADS):
            hr = slice(hh * C, (hh + 1) * C)
            vc = slice(hh * GLA_DV, (hh + 1) * GLA_DV)
            o_h = _dot(scores[hr], v[:, vc]) + inter[hr]
            o_h = _rms(o_h, gain[:, vc])
            gate = r_ref[rows, vc].astype(F32)
            o_ref[rows, vc] = (o_h * (gate * jax.nn.sigmoid(gate))).astype(BF16)

        upd = _dot_tn(v, (k * u_to_end).astype(BF16))
        new_state = state_t * u_from_start[C - 1:C, :]
        for hh in range(GLA_HEADS):
            new_state = new_state + jnp.where(lane_head == hh, upd[hh * GLA_DV:(hh + 1) * GLA_DV], 0.0)
        state_ref[...] = new_state
        return carry

    lax.fori_loop(0, tb // C, chunk, 0)


def _gla(gq, gk, gv, glog, gr, layer, gain):
    S = gq.shape[0]
    tb = GLA_BLOCK
    seg, msk = _gla_tables(GLA_CHUNK)
    resident = seg.size * 2 + msk.size * 4
    streamed = tb * (2 * GLA_K_WIDTH * 2 + 2 * GLA_V_WIDTH * 2 + GLA_K_WIDTH * 4 + GLA_V_WIDTH * 2)
    return pl.pallas_call(
        functools.partial(_gla_kernel, tb=tb),
        grid=(S // tb,),
        in_specs=[_rows(tb, GLA_K_WIDTH), _rows(tb, GLA_K_WIDTH), _rows(tb, GLA_V_WIDTH), _rows(tb, GLA_K_WIDTH),
                  _rows(tb, GLA_V_WIDTH), _layer_block(layer, (1, GLA_V_WIDTH)), _resident(seg.shape),
                  _resident(msk.shape)],
        out_specs=_rows(tb, GLA_V_WIDTH),
        out_shape=jax.ShapeDtypeStruct((S, GLA_V_WIDTH), BF16),
        scratch_shapes=[pltpu.VMEM((GLA_DV, GLA_K_WIDTH), F32)],
        compiler_params=pltpu.CompilerParams(dimension_semantics=("arbitrary",),
                                             vmem_limit_bytes=_vmem_limit(resident, streamed)),
        name="gla",
    )(gq, gk, gv, glog, gr, gain, jnp.asarray(seg, BF16), jnp.asarray(msk, F32))


def _outproj_kernel(a_ref, g_ref, h_ref, wa_ref, wg_ref, post_ref, pre_ref, hout_ref, fin_ref):
    m = _dot(a_ref[...], wa_ref[...]) + _dot(g_ref[...], wg_ref[...])
    h = h_ref[...] + _rms(m, post_ref[...])
    hout_ref[...] = h
    fin_ref[...] = _rms(h, pre_ref[...]).astype(BF16)


def _outproj(a_out, g_out, h, layer, w_out, post_gain, pre_gain):
    S = h.shape[0]
    tm = ROW_TILE
    resident = 2 * MIX_WIDTH * D_MODEL
    streamed = tm * (MIX_WIDTH * 2 + D_MODEL * (4 + 4 + 2))
    return pl.pallas_call(
        _outproj_kernel,
        grid=(S // tm,),
        in_specs=[_rows(tm, ATTN_WIDTH), _rows(tm, GLA_V_WIDTH), _rows(tm, D_MODEL),
                  _layer_block(layer, (ATTN_WIDTH, D_MODEL), (0, 0)),
                  _layer_block(layer, (GLA_V_WIDTH, D_MODEL), (ATTN_WIDTH // GLA_V_WIDTH, 0)),
                  _layer_block(layer, (1, D_MODEL)), _layer_block(layer, (1, D_MODEL))],
        out_specs=[_rows(tm, D_MODEL), _rows(tm, D_MODEL)],
        out_shape=[jax.ShapeDtypeStruct((S, D_MODEL), F32), jax.ShapeDtypeStruct((S, D_MODEL), BF16)],
        compiler_params=pltpu.CompilerParams(dimension_semantics=("arbitrary",),
                                             vmem_limit_bytes=_vmem_limit(resident, streamed)),
        name="outproj",
    )(a_out, g_out, h, w_out, w_out, post_gain, pre_gain)


def _ffn_kernel(x_ref, h_ref, wg_ref, wu_ref, wd_ref, post_ref, out_ref):
    x = x_ref[...]
    acc = None
    for c0 in range(0, D_FF, FF_CHUNK):
        cols = slice(c0, c0 + FF_CHUNK)
        gate = _dot(x, wg_ref[:, cols])
        act = (gate * jax.nn.sigmoid(gate) * _dot(x, wu_ref[:, cols])).astype(BF16)
        part = _dot(act, wd_ref[cols, :])
        acc = part if acc is None else acc + part
    out_ref[...] = h_ref[...] + _rms(acc, post_ref[...])


def _ffn(f_in, h, layer, w_gate, w_up, w_down, post_gain):
    S = h.shape[0]
    tm = ROW_TILE
    resident = 2 * 3 * D_MODEL * D_FF
    streamed = tm * D_MODEL * (2 + 4 + 4)
    return pl.pallas_call(
        _ffn_kernel,
        grid=(S // tm,),
        in_specs=[_rows(tm, D_MODEL), _rows(tm, D_MODEL), _layer_block(layer, (D_MODEL, D_FF)),
                  _layer_block(layer, (D_MODEL, D_FF)), _layer_block(layer, (D_FF, D_MODEL)),
                  _layer_block(layer, (1, D_MODEL))],
        out_specs=_rows(tm, D_MODEL),
        out_shape=jax.ShapeDtypeStruct((S, D_MODEL), F32),
        compiler_params=pltpu.CompilerParams(dimension_semantics=("arbitrary",),
                                             vmem_limit_bytes=_vmem_limit(resident, streamed)),
        name="ffn",
    )(f_in, h, w_gate, w_up, w_down, post_gain)


def _rope_inv_freq():
    inv = np.zeros((1, LANES), np.float32)
    freqs = np.asarray(ROPE_THETA, np.float32) ** (-(np.arange(ROPE_HALF, dtype=np.float32) * 2.0) / ROPE_DIM)
    for lane in range(LANES):
        if lane % HEAD_DIM < ROPE_DIM:
            inv[0, lane] = freqs[lane % ROPE_HALF]
    return jnp.asarray(inv)


def kernel(x, mix_pre_norm, mix_post_norm, ffn_pre_norm, ffn_post_norm, w_in, gla_w_gate_up, gla_b_gate,
           gla_out_norm, attn_out_norm, w_out, w_gate, w_up, w_down):
    B, S, D = x.shape
    depth = mix_pre_norm.shape[0]
    assert D == D_MODEL and S % (max(DILATIONS) * ATTN_SUB) == 0 and S % GLA_BLOCK == 0 and S % ROW_TILE == 0
    inv_freq = _rope_inv_freq()
    gains = lambda t: t.reshape(depth, 1, -1)
    mix_pre, mix_post, ffn_pre, ffn_post = gains(mix_pre_norm), gains(mix_post_norm), gains(ffn_pre_norm), gains(ffn_post_norm)
    gla_norm, attn_norm, b_gate = gains(gla_out_norm), gains(attn_out_norm), gains(gla_b_gate)
    w_in_b = w_in.astype(BF16)
    w_low = jnp.pad(w_in[:, :, MAIN_WIDTH:], ((0, 0), (0, 0), (0, LANES - GLA_GATE_RANK))).astype(BF16)
    w_gu = jnp.pad(gla_w_gate_up, ((0, 0), (0, LANES - GLA_GATE_RANK), (0, 0))).astype(BF16)
    w_out_b, w_gate_b, w_up_b, w_down_b = (t.astype(BF16) for t in (w_out, w_gate, w_up, w_down))

    outs = []
    for b in range(B):
        h = x[b]
        for l in range(depth):
            res = _inproj(h, l, mix_pre, w_in_b, w_low, w_gu, b_gate, inv_freq)
            n_d = len(DILATIONS)
            qs, ks, vs = res[0:n_d], res[n_d:2 * n_d], res[2 * n_d:3 * n_d]
            gq, gk, gv, gr, glog = res[3 * n_d:]
            pats = [_dilattn(qs[i], ks[i], vs[i], d) for i, d in enumerate(DILATIONS)]
            a_out = _attnmerge([p[0] for p in pats], [p[1] for p in pats], l, attn_norm)
            g_out = _gla(gq, gk, gv, glog, gr, l, gla_norm)
            h, f_in = _outproj(a_out, g_out, h, l, w_out_b, mix_post, ffn_pre)
            h = _ffn(f_in, h, l, w_gate_b, w_up_b, w_down_b, ffn_post)
        outs.append(h)
    return jnp.stack(outs, axis=0)
```

```python
import functools

import numpy as np
import jax
import jax.numpy as jnp
from jax import lax
from jax.experimental import pallas as pl
from jax.experimental.pallas import tpu as pltpu

F32 = jnp.float32
BF16 = jnp.bfloat16

D_MODEL = 1024
HEAD_DIM = 64
ATTN_HEADS = 8
ATTN_WIDTH = ATTN_HEADS * HEAD_DIM
GLA_HEADS = 4
GLA_DK = 64
GLA_DV = 128
GLA_K_WIDTH = GLA_HEADS * GLA_DK
GLA_V_WIDTH = GLA_HEADS * GLA_DV
GLA_GATE_RANK = 16
GLA_TAU = 16.0
MIX_WIDTH = ATTN_WIDTH + GLA_V_WIDTH
D_FF = 2816
ROPE_THETA = 500000.0
ROPE_DIM = HEAD_DIM // 4
ROPE_HALF = ROPE_DIM // 2
DILATIONS = (1, 4, 16)
WINDOW_BACK = 128
RMS_EPS = 1e-6
MAIN_WIDTH = 3 * ATTN_WIDTH + 2 * GLA_K_WIDTH + 2 * GLA_V_WIDTH

LANES = 128
MXU_DIM = 256
VMEM_BYTES_V7X = 64 * 1024 * 1024
VMEM_REQUEST_FLOOR = 52 * 1024 * 1024

ROW_TILE = 512
ATTN_Q_TILE = 1024
ATTN_SUB = WINDOW_BACK
HEADS_PER_GROUP = MXU_DIM // HEAD_DIM
ATTN_GROUPS = ATTN_WIDTH // MXU_DIM
ATTN_SLABS = ATTN_WIDTH // LANES
STAT_SUM_LANE = ATTN_HEADS
GLA_CHUNK = 128
GLA_BLOCK = 512
FF_CHUNK = 256
NEG_BIG = -1e30


def _dot(a, b):
    return jnp.dot(a, b, preferred_element_type=F32)


def _dot_nt(a, b):
    return lax.dot_general(a, b, (((1,), (1,)), ((), ())), preferred_element_type=F32)


def _dot_tn(a, b):
    return lax.dot_general(a, b, (((0,), (0,)), ((), ())), preferred_element_type=F32)


def _rms(x, gain):
    return x * lax.rsqrt(jnp.mean(x * x, axis=-1, keepdims=True) + RMS_EPS) * gain


def _split3(x):
    hi = x.astype(BF16)
    r1 = x - hi.astype(F32)
    mid = r1.astype(BF16)
    lo = (r1 - mid.astype(F32)).astype(BF16)
    return hi, mid, lo


def _vmem_limit(resident_bytes, streamed_bytes):
    need = 2 * (resident_bytes + 2 * streamed_bytes)
    return int(min(max(need, VMEM_REQUEST_FLOOR), VMEM_BYTES_V7X - 8 * 1024 * 1024))


def _resident(shape):
    nd = len(shape)
    return pl.BlockSpec(shape, lambda *_: (0,) * nd, pipeline_mode=pl.Buffered(1))


def _layer_block(layer, shape, index=None):
    index = (0,) * len(shape) if index is None else index
    return pl.BlockSpec((None,) + tuple(shape), lambda *_: (layer,) + tuple(index), pipeline_mode=pl.Buffered(1))


def _rows(tm, width):
    return pl.BlockSpec((tm, width), lambda i: (i, 0))


def _rope_offsets_kernel(inv_ref, rot_ref):
    off = lax.broadcasted_iota(jnp.int32, rot_ref.shape[1:], 0).astype(F32) * inv_ref[...]
    rot_ref[0] = jnp.cos(off)
    rot_ref[1] = jnp.sin(off)


def _rope_offsets(inv_freq, tm):
    return pl.pallas_call(_rope_offsets_kernel, out_shape=jax.ShapeDtypeStruct((2, tm, LANES), F32),
                          name="ropeoffsets")(inv_freq)


def _inproj_kernel(x_ref, gain_ref, w_ref, wlow_ref, wup_ref, bias_ref, inv_ref, rot_ref, *refs, tm):
    qkv_refs = refs[:3 * len(DILATIONS)]
    gq_ref, gk_ref, gv_ref, gr_ref, glog_ref, slab_ref = refs[3 * len(DILATIONS):]
    xb = _rms(x_ref[...], gain_ref[...]).astype(BF16)

    def proj(c0, width):
        return _dot(xb, w_ref[:, c0:c0 + width])

    base = (pl.program_id(0) * tm).astype(F32) * inv_ref[...]
    cos_b, sin_b = jnp.cos(base), jnp.sin(base)
    cos = cos_b * rot_ref[0] - sin_b * rot_ref[1]
    sin = sin_b * rot_ref[0] + cos_b * rot_ref[1]
    lane = lax.broadcasted_iota(jnp.int32, (tm, LANES), 1) % HEAD_DIM
    cosf = jnp.concatenate([cos] * ATTN_SLABS, axis=1)
    sin_first = jnp.concatenate([jnp.where(lane < ROPE_HALF, -sin, 0.0)] * ATTN_SLABS, axis=1)
    sin_second = jnp.concatenate([jnp.where(lane >= ROPE_HALF, sin, 0.0)] * ATTN_SLABS, axis=1)

    def rope(t):
        return (t * cosf + pltpu.roll(t, ATTN_WIDTH - ROPE_HALF, 1) * sin_first
                + pltpu.roll(t, ROPE_HALF, 1) * sin_second)

    def emit_layouts(t, out_refs):
        for s in range(ATTN_SLABS):
            slab_ref[0, s] = t[:, s * LANES:(s + 1) * LANES]
        src, d_prev = 0, 1
        for d, out_ref in zip(DILATIONS, out_refs):
            if d == 1:
                out_ref[...] = t.astype(BF16)
                continue
            n, n_prev, ratio = tm // d, tm // d_prev, d // d_prev
            keep = d != DILATIONS[-1]
            for r in range(d):
                start = (r % d_prev) * n_prev + r // d_prev
                part = jnp.concatenate([slab_ref[src, s, pl.ds(start, n, stride=ratio), :]
                                        for s in range(ATTN_SLABS)], axis=1)
                out_ref[:, r * ATTN_WIDTH:(r + 1) * ATTN_WIDTH] = part.astype(BF16)
                if keep:
                    for s in range(ATTN_SLABS):
                        slab_ref[1 - src, s, r * n:(r + 1) * n, :] = part[:, s * LANES:(s + 1) * LANES]
            src, d_prev = 1 - src, d

    n_d = len(DILATIONS)
    c0 = 0
    emit_layouts(rope(proj(c0, ATTN_WIDTH)) * (HEAD_DIM ** -0.5), qkv_refs[0:n_d])
    c0 += ATTN_WIDTH
    emit_layouts(rope(proj(c0, ATTN_WIDTH)), qkv_refs[n_d:2 * n_d])
    c0 += ATTN_WIDTH
    emit_layouts(proj(c0, ATTN_WIDTH), qkv_refs[2 * n_d:3 * n_d])
    c0 += ATTN_WIDTH
    gq_ref[...] = (proj(c0, GLA_K_WIDTH) * (GLA_DK ** -0.5)).astype(BF16)
    c0 += GLA_K_WIDTH
    gk_ref[...] = proj(c0, GLA_K_WIDTH).astype(BF16)
    c0 += GLA_K_WIDTH
    gv_ref[...] = proj(c0, GLA_V_WIDTH).astype(BF16)
    c0 += GLA_V_WIDTH
    gr_ref[...] = proj(c0, GLA_V_WIDTH).astype(BF16)

    g_low = _dot(xb, wlow_ref[...])
    z = _dot(g_low.astype(BF16), wup_ref[...]) + bias_ref[...]
    log_sig = jnp.minimum(z, 0.0) - jnp.log1p(jnp.exp(-jnp.abs(z)))
    glog_ref[...] = log_sig * (1.0 / GLA_TAU)


def _inproj(h, layer, gain, w_in, w_low, w_up, bias, inv_freq, rope_offsets):
    S = h.shape[0]
    tm = rope_offsets.shape[1]
    qkv_specs, qkv_shapes = [], []
    for _ in range(3):
        for d in DILATIONS:
            qkv_specs.append(_rows(tm // d, d * ATTN_WIDTH))
            qkv_shapes.append(jax.ShapeDtypeStruct((S // d, d * ATTN_WIDTH), BF16))
    gla_widths = (GLA_K_WIDTH,) * 2 + (GLA_V_WIDTH,) * 2
    out_shape = qkv_shapes + [jax.ShapeDtypeStruct((S, w), BF16) for w in gla_widths] + [jax.ShapeDtypeStruct((S, GLA_K_WIDTH), F32)]
    resident = 2 * (D_MODEL * MAIN_WIDTH + 2 * D_MODEL * LANES)
    streamed = tm * (D_MODEL * 4 + 3 * len(DILATIONS) * ATTN_WIDTH * 2 + MAIN_WIDTH * 2) + tm * MAIN_WIDTH * 4
    return pl.pallas_call(
        functools.partial(_inproj_kernel, tm=tm),
        grid=(S // tm,),
        in_specs=[_rows(tm, D_MODEL), _layer_block(layer, (1, D_MODEL)), _layer_block(layer, (D_MODEL, MAIN_WIDTH)),
                  _layer_block(layer, (D_MODEL, LANES)), _layer_block(layer, (LANES, GLA_K_WIDTH)),
                  _layer_block(layer, (1, GLA_K_WIDTH)), _resident((1, LANES)), _resident((2, tm, LANES))],
        out_specs=qkv_specs + [_rows(tm, w) for w in gla_widths] + [_rows(tm, GLA_K_WIDTH)],
        out_shape=out_shape,
        scratch_shapes=[pltpu.VMEM((2, ATTN_SLABS, tm, LANES), F32)],
        compiler_params=pltpu.CompilerParams(dimension_semantics=("arbitrary",),
                                             vmem_limit_bytes=_vmem_limit(resident, streamed)),
        name="inproj",
    )(h, gain, w_in, w_low, w_up, bias, inv_freq, rope_offsets)


def _dilattn_kernel(q_ref, kc_ref, kp_ref, vc_ref, vp_ref, o_ref, stat_ref, *, tq):
    first_tile = pl.program_id(1) == 0
    sub = ATTN_SUB
    grp = MXU_DIM
    a = lax.broadcasted_iota(jnp.int32, (sub, 2 * sub), 0)
    c = lax.broadcasted_iota(jnp.int32, (sub, 2 * sub), 1)
    band = (c >= a) & (c <= a + WINDOW_BACK)
    band_first = band & ((c >= sub) | jnp.logical_not(first_tile))
    q_head = lax.broadcasted_iota(jnp.int32, (sub, grp), 1) // HEAD_DIM
    stat_lane = lax.broadcasted_iota(jnp.int32, (sub, LANES), 1)

    for j in range(tq // sub):
        rows = slice(j * sub, (j + 1) * sub)
        valid = jnp.concatenate([band_first if j == 0 else band] * HEADS_PER_GROUP, axis=0)
        stats = jnp.zeros((sub, LANES), F32)
        for g in range(ATTN_GROUPS):
            cols = slice(g * grp, (g + 1) * grp)
            qg = q_ref[rows, cols]
            if j == 0:
                kg = jnp.concatenate([kp_ref[:, cols], kc_ref[0:sub, cols]], axis=0)
                vg = jnp.concatenate([vp_ref[:, cols], vc_ref[0:sub, cols]], axis=0)
            else:
                kg = kc_ref[(j - 1) * sub:(j + 1) * sub, cols]
                vg = vc_ref[(j - 1) * sub:(j + 1) * sub, cols]
            qs = jnp.concatenate([jnp.where(q_head == hh, qg, jnp.zeros_like(qg))
                                  for hh in range(HEADS_PER_GROUP)], axis=0)
            s = jnp.where(valid, _dot_nt(qs, kg), NEG_BIG)
            m = jnp.max(s, axis=-1, keepdims=True)
            p = jnp.exp(s - m)
            sums = jnp.sum(p, axis=-1, keepdims=True)
            pv = _dot(p.astype(BF16), vg)
            o_grp = jnp.zeros((sub, grp), F32)
            for hh in range(HEADS_PER_GROUP):
                hr = slice(hh * sub, (hh + 1) * sub)
                head = g * HEADS_PER_GROUP + hh
                o_grp = jnp.where(q_head == hh, pv[hr], o_grp)
                stats = jnp.where(stat_lane == head, m[hr], stats)
                stats = jnp.where(stat_lane == STAT_SUM_LANE + head, sums[hr], stats)
            o_ref[rows, cols] = o_grp
        stat_ref[rows, :] = stats


def _dilattn(q, k, v, d):
    rows = q.shape[0]
    tq = min(ATTN_Q_TILE, rows)
    W = ATTN_WIDTH
    cur = pl.BlockSpec((tq, W), lambda r, i: (i, r))
    prev = pl.BlockSpec((ATTN_SUB, W), lambda r, i: (jnp.maximum(i * (tq // ATTN_SUB) - 1, 0), r))
    streamed = tq * W * (3 * 2 + 4) + 2 * ATTN_SUB * W * 2 + tq * LANES * 4
    return pl.pallas_call(
        functools.partial(_dilattn_kernel, tq=tq),
        grid=(d, rows // tq),
        in_specs=[cur, cur, prev, cur, prev],
        out_specs=[pl.BlockSpec((tq, W), lambda r, i: (i, r)), pl.BlockSpec((tq, LANES), lambda r, i: (i, r))],
        out_shape=[jax.ShapeDtypeStruct((rows, d * W), F32), jax.ShapeDtypeStruct((rows, d * LANES), F32)],
        compiler_params=pltpu.CompilerParams(dimension_semantics=("arbitrary", "arbitrary"),
                                             vmem_limit_bytes=_vmem_limit(0, streamed)),
        name=f"dilattn{d}",
    )(q, k, k, v, v)


def _attnmerge_kernel(*refs, tm):
    n_d = len(DILATIONS)
    o_refs, st_refs = refs[:n_d], refs[n_d:2 * n_d]
    expand_ref, gain_ref, out_ref, oslab_ref, stslab_ref = refs[2 * n_d:]

    def natural_rows(ref, d, width, slab_ref):
        if d == 1:
            return ref[...]
        n = tm // d
        slabs = width // LANES
        for r in range(d):
            for s in range(slabs):
                slab_ref[s, pl.ds(r, n, stride=d), :] = ref[:, r * width + s * LANES:r * width + (s + 1) * LANES]
        return jnp.concatenate([slab_ref[s] for s in range(slabs)], axis=1)

    stats = [natural_rows(st_refs[i], d, LANES, stslab_ref.at[i]) for i, d in enumerate(DILATIONS)]
    lane = lax.broadcasted_iota(jnp.int32, (tm, LANES), 1)
    top = functools.reduce(jnp.maximum, stats)
    scale = [jnp.exp(st - top) for st in stats]
    total = sum(sc * pltpu.roll(st, LANES - STAT_SUM_LANE, 1) for sc, st in zip(scale, stats))
    inv_total = jnp.where(lane < ATTN_HEADS, 1.0 / total, 0.0)
    acc = None
    for i, d in enumerate(DILATIONS):
        parts = _split3(scale[i] * inv_total)
        coef = sum(_dot(p, expand_ref[...]) for p in parts)
        term = coef * natural_rows(o_refs[i], d, ATTN_WIDTH, oslab_ref)
        acc = term if acc is None else acc + term
    out_ref[...] = _rms(acc, gain_ref[...]).astype(BF16)


def _attnmerge(os_, stats, layer, gain):
    S = os_[0].shape[0]
    tm = ROW_TILE
    expand = np.zeros((LANES, ATTN_WIDTH), np.float32)
    for h in range(ATTN_HEADS):
        expand[h, h * HEAD_DIM:(h + 1) * HEAD_DIM] = 1.0
    streamed = tm * (3 * ATTN_WIDTH * 4 + 3 * LANES * 4 + ATTN_WIDTH * 2)
    scratch_bytes = (ATTN_SLABS + len(DILATIONS)) * tm * LANES * 4
    return pl.pallas_call(
        functools.partial(_attnmerge_kernel, tm=tm),
        grid=(S // tm,),
        in_specs=([_rows(tm // d, d * ATTN_WIDTH) for d in DILATIONS] + [_rows(tm // d, d * LANES) for d in DILATIONS]
                  + [_resident((LANES, ATTN_WIDTH)), _layer_block(layer, (1, ATTN_WIDTH))]),
        out_specs=_rows(tm, ATTN_WIDTH),
        out_shape=jax.ShapeDtypeStruct((S, ATTN_WIDTH), BF16),
        scratch_shapes=[pltpu.VMEM((ATTN_SLABS, tm, LANES), F32), pltpu.VMEM((len(DILATIONS), 1, tm, LANES), F32)],
        compiler_params=pltpu.CompilerParams(dimension_semantics=("arbitrary",),
                                             vmem_limit_bytes=_vmem_limit(scratch_bytes, streamed)),
        name="attnmerge",
    )(*os_, *stats, jnp.asarray(expand, BF16), gain)


def _gla_tables(C):
    levels = int(np.log2(C))
    t = np.arange(C)[:, None]
    j = np.arange(C)[None, :]
    masks = []
    for l in range(levels):
        c = 1 << l
        second = (t % (2 * c)) >= c
        masks.append((t // (2 * c) == j // (2 * c)) & second & ((j % (2 * c)) < c))
    masks.append(t == j)
    seg = (j <= t).astype(np.float32)
    msk = np.stack([np.tile(mm, (GLA_HEADS, 1)) for mm in masks]).astype(np.float32)
    return seg, msk


def _gla_kernel(q_ref, k_ref, v_ref, g_ref, r_ref, gain_ref, seg_ref, msk_ref, o_ref, state_ref, *, tb):
    C = GLA_CHUNK
    levels = msk_ref.shape[0] - 1
    lane_head = lax.broadcasted_iota(jnp.int32, (C, GLA_K_WIDTH), 1) // GLA_DK
    row = lax.broadcasted_iota(jnp.int32, (C, GLA_K_WIDTH), 0)

    @pl.when(pl.program_id(0) == 0)
    def _():
        state_ref[...] = jnp.zeros_like(state_ref)

    def stack_heads(t):
        return jnp.concatenate([jnp.where(lane_head == hh, t, jnp.zeros_like(t)) for hh in range(GLA_HEADS)], axis=0)

    for ci in range(tb // C):
        rows = slice(ci * C, (ci + 1) * C)
        q = q_ref[rows, :].astype(F32)
        k = k_ref[rows, :].astype(F32)
        v = v_ref[rows, :]
        b = sum(_dot(seg_ref[...], part) for part in _split3(g_ref[rows, :]))
        u_from_start = jnp.exp(jnp.minimum(b, 0.0))
        u_to_end = jnp.exp(jnp.minimum(b[C - 1:C, :] - b, 0.0))

        scores = msk_ref[levels] * _dot_nt(stack_heads(q.astype(BF16)), k.astype(BF16))
        block_end = b
        for l in range(levels):
            c = 1 << l
            first_half = (row & c) == 0
            b_mid = jnp.where(first_half, block_end, pltpu.roll(block_end, c, 0))
            ul = jnp.exp(-jnp.abs(b - b_mid))
            scores = scores + msk_ref[l] * _dot_nt(stack_heads((q * ul).astype(BF16)), (k * ul).astype(BF16))
            if l + 1 < levels:
                block_end = jnp.where(first_half, pltpu.roll(block_end, C - c, 0), block_end)

        state_t = state_ref[...]
        inter = _dot_nt(stack_heads((q * u_from_start).astype(BF16)), state_t.astype(BF16))
        scores = scores.astype(BF16)
        gain = gain_ref[...]
        for hh in range(GLA_HEADS):
            hr = slice(hh * C, (hh + 1) * C)
            vc = slice(hh * GLA_DV, (hh + 1) * GLA_DV)
            o_h = _dot(scores[hr], v[:, vc]) + inter[hr]
            o_h = _rms(o_h, gain[:, vc])
            gate = r_ref[rows, vc].astype(F32)
            o_ref[rows, vc] = (o_h * (gate * jax.nn.sigmoid(gate))).astype(BF16)

        upd = _dot_tn(v, (k * u_to_end).astype(BF16))
        new_state = state_t * u_from_start[C - 1:C, :]
        for hh in range(GLA_HEADS):
            new_state = new_state + jnp.where(lane_head == hh, upd[hh * GLA_DV:(hh + 1) * GLA_DV], 0.0)
        state_ref[...] = new_state


def _gla(gq, gk, gv, glog, gr, layer, gain):
    S = gq.shape[0]
    tb = GLA_BLOCK
    seg, msk = _gla_tables(GLA_CHUNK)
    resident = seg.size * 2 + msk.size * 4
    streamed = tb * (2 * GLA_K_WIDTH * 2 + 2 * GLA_V_WIDTH * 2 + GLA_K_WIDTH * 4 + GLA_V_WIDTH * 2)
    return pl.pallas_call(
        functools.partial(_gla_kernel, tb=tb),
        grid=(S // tb,),
        in_specs=[_rows(tb, GLA_K_WIDTH), _rows(tb, GLA_K_WIDTH), _rows(tb, GLA_V_WIDTH), _rows(tb, GLA_K_WIDTH),
                  _rows(tb, GLA_V_WIDTH), _layer_block(layer, (1, GLA_V_WIDTH)), _resident(seg.shape),
                  _resident(msk.shape)],
        out_specs=_rows(tb, GLA_V_WIDTH),
        out_shape=jax.ShapeDtypeStruct((S, GLA_V_WIDTH), BF16),
        scratch_shapes=[pltpu.VMEM((GLA_DV, GLA_K_WIDTH), F32)],
        compiler_params=pltpu.CompilerParams(dimension_semantics=("arbitrary",),
                                             vmem_limit_bytes=_vmem_limit(resident, streamed)),
        name="gla",
    )(gq, gk, gv, glog, gr, gain, jnp.asarray(seg, BF16), jnp.asarray(msk, F32))


def _outproj_kernel(a_ref, g_ref, h_ref, wa_ref, wg_ref, post_ref, pre_ref, hout_ref, fin_ref):
    m = _dot(a_ref[...], wa_ref[...]) + _dot(g_ref[...], wg_ref[...])
    h = h_ref[...] + _rms(m, post_ref[...])
    hout_ref[...] = h
    fin_ref[...] = _rms(h, pre_ref[...]).astype(BF16)


def _outproj(a_out, g_out, h, layer, w_out, post_gain, pre_gain):
    S = h.shape[0]
    tm = ROW_TILE
    resident = 2 * MIX_WIDTH * D_MODEL
    streamed = tm * (MIX_WIDTH * 2 + D_MODEL * (4 + 4 + 2))
    return pl.pallas_call(
        _outproj_kernel,
        grid=(S // tm,),
        in_specs=[_rows(tm, ATTN_WIDTH), _rows(tm, GLA_V_WIDTH), _rows(tm, D_MODEL),
                  _layer_block(layer, (ATTN_WIDTH, D_MODEL), (0, 0)),
                  _layer_block(layer, (GLA_V_WIDTH, D_MODEL), (ATTN_WIDTH // GLA_V_WIDTH, 0)),
                  _layer_block(layer, (1, D_MODEL)), _layer_block(layer, (1, D_MODEL))],
        out_specs=[_rows(tm, D_MODEL), _rows(tm, D_MODEL)],
        out_shape=[jax.ShapeDtypeStruct((S, D_MODEL), F32), jax.ShapeDtypeStruct((S, D_MODEL), BF16)],
        compiler_params=pltpu.CompilerParams(dimension_semantics=("arbitrary",),
                                             vmem_limit_bytes=_vmem_limit(resident, streamed)),
        name="outproj",
    )(a_out, g_out, h, w_out, w_out, post_gain, pre_gain)


def _ffn_kernel(x_ref, h_ref, wg_ref, wu_ref, wd_ref, post_ref, out_ref):
    x = x_ref[...]
    acc = None
    for c0 in range(0, D_FF, FF_CHUNK):
        cols = slice(c0, c0 + FF_CHUNK)
        gate = _dot(x, wg_ref[:, cols])
        act = (gate * jax.nn.sigmoid(gate) * _dot(x, wu_ref[:, cols])).astype(BF16)
        part = _dot(act, wd_ref[cols, :])
        acc = part if acc is None else acc + part
    out_ref[...] = h_ref[...] + _rms(acc, post_ref[...])


def _ffn(f_in, h, layer, w_gate, w_up, w_down, post_gain):
    S = h.shape[0]
    tm = ROW_TILE
    resident = 2 * 3 * D_MODEL * D_FF
    streamed = tm * D_MODEL * (2 + 4 + 4)
    return pl.pallas_call(
        _ffn_kernel,
        grid=(S // tm,),
        in_specs=[_rows(tm, D_MODEL), _rows(tm, D_MODEL), _layer_block(layer, (D_MODEL, D_FF)),
                  _layer_block(layer, (D_MODEL, D_FF)), _layer_block(layer, (D_FF, D_MODEL)),
                  _layer_block(layer, (1, D_MODEL))],
        out_specs=_rows(tm, D_MODEL),
        out_shape=jax.ShapeDtypeStruct((S, D_MODEL), F32),
        compiler_params=pltpu.CompilerParams(dimension_semantics=("arbitrary",),
                                             vmem_limit_bytes=_vmem_limit(resident, streamed)),
        name="ffn",
    )(f_in, h, w_gate, w_up, w_down, post_gain)


def _rope_inv_freq():
    inv = np.zeros((1, LANES), np.float32)
    freqs = np.asarray(ROPE_THETA, np.float32) ** (-(np.arange(ROPE_HALF, dtype=np.float32) * 2.0) / ROPE_DIM)
    for lane in range(LANES):
        if lane % HEAD_DIM < ROPE_DIM:
            inv[0, lane] = freqs[lane % ROPE_HALF]
    return jnp.asarray(inv)


def kernel(x, mix_pre_norm, mix_post_norm, ffn_pre_norm, ffn_post_norm, w_in, gla_w_gate_up, gla_b_gate,
           gla_out_norm, attn_out_norm, w_out, w_gate, w_up, w_down):
    B, S, D = x.shape
    depth = mix_pre_norm.shape[0]
    assert D == D_MODEL and S % (max(DILATIONS) * ATTN_SUB) == 0 and S % GLA_BLOCK == 0 and S % ROW_TILE == 0
    assert DILATIONS[0] == 1 and all(b % a == 0 for a, b in zip(DILATIONS, DILATIONS[1:]))
    inv_freq = _rope_inv_freq()
    rope_offsets = _rope_offsets(inv_freq, ROW_TILE)
    gains = lambda t: t.reshape(depth, 1, -1)
    mix_pre, mix_post, ffn_pre, ffn_post = gains(mix_pre_norm), gains(mix_post_norm), gains(ffn_pre_norm), gains(ffn_post_norm)
    gla_norm, attn_norm, b_gate = gains(gla_out_norm), gains(attn_out_norm), gains(gla_b_gate)
    w_in_b = w_in.astype(BF16)
    w_low = jnp.pad(w_in[:, :, MAIN_WIDTH:], ((0, 0), (0, 0), (0, LANES - GLA_GATE_RANK))).astype(BF16)
    w_gu = jnp.pad(gla_w_gate_up, ((0, 0), (0, LANES - GLA_GATE_RANK), (0, 0))).astype(BF16)
    w_out_b, w_gate_b, w_up_b, w_down_b = (t.astype(BF16) for t in (w_out, w_gate, w_up, w_down))

    outs = []
    for b in range(B):
        h = x[b]
        for l in range(depth):
            res = _inproj(h, l, mix_pre, w_in_b, w_low, w_gu, b_gate, inv_freq, rope_offsets)
            n_d = len(DILATIONS)
            qs, ks, vs = res[0:n_d], res[n_d:2 * n_d], res[2 * n_d:3 * n_d]
            gq, gk, gv, gr, glog = res[3 * n_d:]
            pats = [_dilattn(qs[i], ks[i], vs[i], d) for i, d in enumerate(DILATIONS)]
            a_out = _attnmerge([p[0] for p in pats], [p[1] for p in pats], l, attn_norm)
            g_out = _gla(gq, gk, gv, glog, gr, l, gla_norm)
            h, f_in = _outproj(a_out, g_out, h, l, w_out_b, mix_post, ffn_pre)
            h = _ffn(f_in, h, l, w_gate_b, w_up_b, w_down_b, ffn_post)
        outs.append(h)
    return jnp.stack(outs, axis=0)
```

```python
import functools

import numpy as np
import jax
import jax.numpy as jnp
from jax import lax
from jax.experimental import pallas as pl
from jax.experimental.pallas import tpu as pltpu

F32 = jnp.float32
BF16 = jnp.bfloat16

D_MODEL = 1024
HEAD_DIM = 64
ATTN_HEADS = 8
ATTN_WIDTH = ATTN_HEADS * HEAD_DIM
GLA_HEADS = 4
GLA_DK = 64
GLA_DV = 128
GLA_K_WIDTH = GLA_HEADS * GLA_DK
GLA_V_WIDTH = GLA_HEADS * GLA_DV
GLA_GATE_RANK = 16
GLA_TAU = 16.0
MIX_WIDTH = ATTN_WIDTH + GLA_V_WIDTH
D_FF = 2816
ROPE_THETA = 500000.0
ROPE_DIM = HEAD_DIM // 4
ROPE_HALF = ROPE_DIM // 2
DILATIONS = (1, 4, 16)
WINDOW_BACK = 128
RMS_EPS = 1e-6
MAIN_WIDTH = 3 * ATTN_WIDTH + 2 * GLA_K_WIDTH + 2 * GLA_V_WIDTH

LANES = 128
MXU_DIM = 256
VMEM_BYTES_V7X = 64 * 1024 * 1024
VMEM_REQUEST_FLOOR = 52 * 1024 * 1024

ROW_TILE = 512
ATTN_Q_TILE = 1024
ATTN_SUB = WINDOW_BACK
HEADS_PER_GROUP = MXU_DIM // HEAD_DIM
ATTN_GROUPS = ATTN_WIDTH // MXU_DIM
ATTN_SLABS = ATTN_WIDTH // LANES
STAT_SUM_LANE = ATTN_HEADS
GLA_CHUNK = 128
GLA_BLOCK = 512
FF_CHUNK = 256
NEG_BIG = -1e30
MERGE_COEF_PIECES = 1


def _dot(a, b):
    return jnp.dot(a, b, preferred_element_type=F32)


def _dot_nt(a, b):
    return lax.dot_general(a, b, (((1,), (1,)), ((), ())), preferred_element_type=F32)


def _dot_tn(a, b):
    return lax.dot_general(a, b, (((0,), (0,)), ((), ())), preferred_element_type=F32)


def _rms(x, gain):
    return x * lax.rsqrt(jnp.mean(x * x, axis=-1, keepdims=True) + RMS_EPS) * gain


def _split3(x):
    hi = x.astype(BF16)
    r1 = x - hi.astype(F32)
    mid = r1.astype(BF16)
    lo = (r1 - mid.astype(F32)).astype(BF16)
    return hi, mid, lo


def _vmem_limit(resident_bytes, streamed_bytes):
    need = 2 * (resident_bytes + 2 * streamed_bytes)
    return int(min(max(need, VMEM_REQUEST_FLOOR), VMEM_BYTES_V7X - 8 * 1024 * 1024))


def _resident(shape):
    nd = len(shape)
    return pl.BlockSpec(shape, lambda *_: (0,) * nd, pipeline_mode=pl.Buffered(1))


def _layer_block(layer, shape, index=None):
    index = (0,) * len(shape) if index is None else index
    return pl.BlockSpec((None,) + tuple(shape), lambda *_: (layer,) + tuple(index), pipeline_mode=pl.Buffered(1))


def _rows(tm, width):
    return pl.BlockSpec((tm, width), lambda i: (i, 0))


def _rope_offsets_kernel(inv_ref, rot_ref):
    off = lax.broadcasted_iota(jnp.int32, rot_ref.shape[1:], 0).astype(F32) * inv_ref[...]
    rot_ref[0] = jnp.cos(off)
    rot_ref[1] = jnp.sin(off)


def _rope_offsets(inv_freq, tm):
    return pl.pallas_call(_rope_offsets_kernel, out_shape=jax.ShapeDtypeStruct((2, tm, LANES), F32),
                          name="ropeoffsets")(inv_freq)


def _inproj_kernel(x_ref, gain_ref, w_ref, wlow_ref, wup_ref, bias_ref, inv_ref, rot_ref, *refs, tm):
    qkv_refs = refs[:3 * len(DILATIONS)]
    gq_ref, gk_ref, gv_ref, gr_ref, glog_ref, slab_ref = refs[3 * len(DILATIONS):]
    xb = _rms(x_ref[...], gain_ref[...]).astype(BF16)

    def proj(c0, width):
        return _dot(xb, w_ref[:, c0:c0 + width])

    base = (pl.program_id(0) * tm).astype(F32) * inv_ref[...]
    cos_b, sin_b = jnp.cos(base), jnp.sin(base)
    cos = cos_b * rot_ref[0] - sin_b * rot_ref[1]
    sin = sin_b * rot_ref[0] + cos_b * rot_ref[1]
    lane = lax.broadcasted_iota(jnp.int32, (tm, LANES), 1) % HEAD_DIM
    cosf = jnp.concatenate([cos] * ATTN_SLABS, axis=1)
    sin_first = jnp.concatenate([jnp.where(lane < ROPE_HALF, -sin, 0.0)] * ATTN_SLABS, axis=1)
    sin_second = jnp.concatenate([jnp.where(lane >= ROPE_HALF, sin, 0.0)] * ATTN_SLABS, axis=1)

    def rope(t):
        return (t * cosf + pltpu.roll(t, ATTN_WIDTH - ROPE_HALF, 1) * sin_first
                + pltpu.roll(t, ROPE_HALF, 1) * sin_second)

    def emit_layouts(t, out_refs):
        for s in range(ATTN_SLABS):
            slab_ref[0, s] = t[:, s * LANES:(s + 1) * LANES]
        src, d_prev = 0, 1
        for d, out_ref in zip(DILATIONS, out_refs):
            if d == 1:
                out_ref[...] = t.astype(BF16)
                continue
            n, n_prev, ratio = tm // d, tm // d_prev, d // d_prev
            keep = d != DILATIONS[-1]
            for r in range(d):
                start = (r % d_prev) * n_prev + r // d_prev
                part = jnp.concatenate([slab_ref[src, s, pl.ds(start, n, stride=ratio), :]
                                        for s in range(ATTN_SLABS)], axis=1)
                out_ref[:, r * ATTN_WIDTH:(r + 1) * ATTN_WIDTH] = part.astype(BF16)
                if keep:
                    for s in range(ATTN_SLABS):
                        slab_ref[1 - src, s, r * n:(r + 1) * n, :] = part[:, s * LANES:(s + 1) * LANES]
            src, d_prev = 1 - src, d

    n_d = len(DILATIONS)
    c0 = 0
    emit_layouts(rope(proj(c0, ATTN_WIDTH)) * (HEAD_DIM ** -0.5), qkv_refs[0:n_d])
    c0 += ATTN_WIDTH
    emit_layouts(rope(proj(c0, ATTN_WIDTH)), qkv_refs[n_d:2 * n_d])
    c0 += ATTN_WIDTH
    emit_layouts(proj(c0, ATTN_WIDTH), qkv_refs[2 * n_d:3 * n_d])
    c0 += ATTN_WIDTH
    gq_ref[...] = (proj(c0, GLA_K_WIDTH) * (GLA_DK ** -0.5)).astype(BF16)
    c0 += GLA_K_WIDTH
    gk_ref[...] = proj(c0, GLA_K_WIDTH).astype(BF16)
    c0 += GLA_K_WIDTH
    gv_ref[...] = proj(c0, GLA_V_WIDTH).astype(BF16)
    c0 += GLA_V_WIDTH
    gr_ref[...] = proj(c0, GLA_V_WIDTH).astype(BF16)

    g_low = _dot(xb, wlow_ref[...])
    z = _dot(g_low.astype(BF16), wup_ref[...]) + bias_ref[...]
    log_sig = jnp.minimum(z, 0.0) - jnp.log1p(jnp.exp(-jnp.abs(z)))
    glog_ref[...] = log_sig * (1.0 / GLA_TAU)


def _inproj(h, layer, gain, w_in, w_low, w_up, bias, inv_freq, rope_offsets):
    S = h.shape[0]
    tm = rope_offsets.shape[1]
    qkv_specs, qkv_shapes = [], []
    for _ in range(3):
        for d in DILATIONS:
            qkv_specs.append(_rows(tm // d, d * ATTN_WIDTH))
            qkv_shapes.append(jax.ShapeDtypeStruct((S // d, d * ATTN_WIDTH), BF16))
    gla_widths = (GLA_K_WIDTH,) * 2 + (GLA_V_WIDTH,) * 2
    out_shape = qkv_shapes + [jax.ShapeDtypeStruct((S, w), BF16) for w in gla_widths] + [jax.ShapeDtypeStruct((S, GLA_K_WIDTH), F32)]
    resident = 2 * (D_MODEL * MAIN_WIDTH + 2 * D_MODEL * LANES)
    streamed = tm * (D_MODEL * 4 + 3 * len(DILATIONS) * ATTN_WIDTH * 2 + MAIN_WIDTH * 2) + tm * MAIN_WIDTH * 4
    return pl.pallas_call(
        functools.partial(_inproj_kernel, tm=tm),
        grid=(S // tm,),
        in_specs=[_rows(tm, D_MODEL), _layer_block(layer, (1, D_MODEL)), _layer_block(layer, (D_MODEL, MAIN_WIDTH)),
                  _layer_block(layer, (D_MODEL, LANES)), _layer_block(layer, (LANES, GLA_K_WIDTH)),
                  _layer_block(layer, (1, GLA_K_WIDTH)), _resident((1, LANES)), _resident((2, tm, LANES))],
        out_specs=qkv_specs + [_rows(tm, w) for w in gla_widths] + [_rows(tm, GLA_K_WIDTH)],
        out_shape=out_shape,
        scratch_shapes=[pltpu.VMEM((2, ATTN_SLABS, tm, LANES), F32)],
        compiler_params=pltpu.CompilerParams(dimension_semantics=("arbitrary",),
                                             vmem_limit_bytes=_vmem_limit(resident, streamed)),
        name="inproj",
    )(h, gain, w_in, w_low, w_up, bias, inv_freq, rope_offsets)


def _dilattn_kernel(q_ref, kc_ref, kp_ref, vc_ref, vp_ref, o_ref, stat_ref, *, tq):
    first_tile = pl.program_id(1) == 0
    sub = ATTN_SUB
    grp = MXU_DIM
    a = lax.broadcasted_iota(jnp.int32, (sub, 2 * sub), 0)
    c = lax.broadcasted_iota(jnp.int32, (sub, 2 * sub), 1)
    band = (c >= a) & (c <= a + WINDOW_BACK)
    band_first = band & ((c >= sub) | jnp.logical_not(first_tile))
    q_head = lax.broadcasted_iota(jnp.int32, (sub, grp), 1) // HEAD_DIM
    stat_lane = lax.broadcasted_iota(jnp.int32, (sub, LANES), 1)

    for j in range(tq // sub):
        rows = slice(j * sub, (j + 1) * sub)
        valid = jnp.concatenate([band_first if j == 0 else band] * HEADS_PER_GROUP, axis=0)
        stats = jnp.zeros((sub, LANES), F32)
        for g in range(ATTN_GROUPS):
            cols = slice(g * grp, (g + 1) * grp)
            qg = q_ref[rows, cols]
            if j == 0:
                kg = jnp.concatenate([kp_ref[:, cols], kc_ref[0:sub, cols]], axis=0)
                vg = jnp.concatenate([vp_ref[:, cols], vc_ref[0:sub, cols]], axis=0)
            else:
                kg = kc_ref[(j - 1) * sub:(j + 1) * sub, cols]
                vg = vc_ref[(j - 1) * sub:(j + 1) * sub, cols]
            qs = jnp.concatenate([jnp.where(q_head == hh, qg, jnp.zeros_like(qg))
                                  for hh in range(HEADS_PER_GROUP)], axis=0)
            s = jnp.where(valid, _dot_nt(qs, kg), NEG_BIG)
            m = jnp.max(s, axis=-1, keepdims=True)
            p = jnp.exp(s - m)
            sums = jnp.sum(p, axis=-1, keepdims=True)
            pv = _dot(p.astype(BF16), vg)
            o_grp = jnp.zeros((sub, grp), F32)
            for hh in range(HEADS_PER_GROUP):
                hr = slice(hh * sub, (hh + 1) * sub)
                head = g * HEADS_PER_GROUP + hh
                o_grp = jnp.where(q_head == hh, pv[hr], o_grp)
                stats = jnp.where(stat_lane == head, m[hr], stats)
                stats = jnp.where(stat_lane == STAT_SUM_LANE + head, sums[hr], stats)
            o_ref[rows, cols] = o_grp
        stat_ref[rows, :] = stats


def _dilattn(q, k, v, d):
    rows = q.shape[0]
    tq = min(ATTN_Q_TILE, rows)
    W = ATTN_WIDTH
    cur = pl.BlockSpec((tq, W), lambda r, i: (i, r))
    prev = pl.BlockSpec((ATTN_SUB, W), lambda r, i: (jnp.maximum(i * (tq // ATTN_SUB) - 1, 0), r))
    streamed = tq * W * (3 * 2 + 4) + 2 * ATTN_SUB * W * 2 + tq * LANES * 4
    return pl.pallas_call(
        functools.partial(_dilattn_kernel, tq=tq),
        grid=(d, rows // tq),
        in_specs=[cur, cur, prev, cur, prev],
        out_specs=[pl.BlockSpec((tq, W), lambda r, i: (i, r)), pl.BlockSpec((tq, LANES), lambda r, i: (i, r))],
        out_shape=[jax.ShapeDtypeStruct((rows, d * W), F32), jax.ShapeDtypeStruct((rows, d * LANES), F32)],
        compiler_params=pltpu.CompilerParams(dimension_semantics=("arbitrary", "arbitrary"),
                                             vmem_limit_bytes=_vmem_limit(0, streamed)),
        name=f"dilattn{d}",
    )(q, k, k, v, v)


def _merge_patterns(o_refs, st_refs, expand_ref, gain_ref, oslab_ref, stslab_ref, tm):
    def natural_rows(ref, d, width, slab_ref):
        if d == 1:
            return ref[...]
        n = tm // d
        slabs = width // LANES
        for r in range(d):
            for s in range(slabs):
                slab_ref[s, pl.ds(r, n, stride=d), :] = ref[:, r * width + s * LANES:r * width + (s + 1) * LANES]
        return jnp.concatenate([slab_ref[s] for s in range(slabs)], axis=1)

    stats = [natural_rows(st_refs[i], d, LANES, stslab_ref.at[i]) for i, d in enumerate(DILATIONS)]
    lane = lax.broadcasted_iota(jnp.int32, (tm, LANES), 1)
    top = functools.reduce(jnp.maximum, stats)
    scale = [jnp.exp(st - top) for st in stats]
    total = sum(sc * pltpu.roll(st, LANES - STAT_SUM_LANE, 1) for sc, st in zip(scale, stats))
    inv_total = jnp.where(lane < ATTN_HEADS, 1.0 / total, 0.0)
    acc = None
    for i, d in enumerate(DILATIONS):
        parts = _split3(scale[i] * inv_total)[:MERGE_COEF_PIECES]
        coef = sum(_dot(p, expand_ref[...]) for p in parts)
        term = coef * natural_rows(o_refs[i], d, ATTN_WIDTH, oslab_ref)
        acc = term if acc is None else acc + term
    return _rms(acc, gain_ref[...]).astype(BF16)


def _gla_tables(C):
    levels = int(np.log2(C))
    t = np.arange(C)[:, None]
    j = np.arange(C)[None, :]
    masks = []
    for l in range(levels):
        c = 1 << l
        second = (t % (2 * c)) >= c
        masks.append((t // (2 * c) == j // (2 * c)) & second & ((j % (2 * c)) < c))
    masks.append(t == j)
    seg = (j <= t).astype(np.float32)
    msk = np.stack([np.tile(mm, (GLA_HEADS, 1)) for mm in masks]).astype(np.float32)
    return seg, msk


def _gla_kernel(q_ref, k_ref, v_ref, g_ref, r_ref, gain_ref, seg_ref, msk_ref, o_ref, state_ref, *, tb):
    C = GLA_CHUNK
    levels = msk_ref.shape[0] - 1
    lane_head = lax.broadcasted_iota(jnp.int32, (C, GLA_K_WIDTH), 1) // GLA_DK
    row = lax.broadcasted_iota(jnp.int32, (C, GLA_K_WIDTH), 0)

    @pl.when(pl.program_id(0) == 0)
    def _():
        state_ref[...] = jnp.zeros_like(state_ref)

    def stack_heads(t):
        return jnp.concatenate([jnp.where(lane_head == hh, t, jnp.zeros_like(t)) for hh in range(GLA_HEADS)], axis=0)

    for ci in range(tb // C):
        rows = slice(ci * C, (ci + 1) * C)
        q = q_ref[rows, :].astype(F32)
        k = k_ref[rows, :].astype(F32)
        v = v_ref[rows, :]
        b = sum(_dot(seg_ref[...], part) for part in _split3(g_ref[rows, :]))
        u_from_start = jnp.exp(jnp.minimum(b, 0.0))
        u_to_end = jnp.exp(jnp.minimum(b[C - 1:C, :] - b, 0.0))

        scores = msk_ref[levels] * _dot_nt(stack_heads(q.astype(BF16)), k.astype(BF16))
        block_end = b
        for l in range(levels):
            c = 1 << l
            first_half = (row & c) == 0
            b_mid = jnp.where(first_half, block_end, pltpu.roll(block_end, c, 0))
            ul = jnp.exp(-jnp.abs(b - b_mid))
            scores = scores + msk_ref[l] * _dot_nt(stack_heads((q * ul).astype(BF16)), (k * ul).astype(BF16))
            if l + 1 < levels:
                block_end = jnp.where(first_half, pltpu.roll(block_end, C - c, 0), block_end)

        state_t = state_ref[...]
        inter = _dot_nt(stack_heads((q * u_from_start).astype(BF16)), state_t.astype(BF16))
        scores = scores.astype(BF16)
        gain = gain_ref[...]
        for hh in range(GLA_HEADS):
            hr = slice(hh * C, (hh + 1) * C)
            vc = slice(hh * GLA_DV, (hh + 1) * GLA_DV)
            o_h = _dot(scores[hr], v[:, vc]) + inter[hr]
            o_h = _rms(o_h, gain[:, vc])
            gate = r_ref[rows, vc].astype(F32)
            o_ref[rows, vc] = (o_h * (gate * jax.nn.sigmoid(gate))).astype(BF16)

        upd = _dot_tn(v, (k * u_to_end).astype(BF16))
        new_state = state_t * u_from_start[C - 1:C, :]
        for hh in range(GLA_HEADS):
            new_state = new_state + jnp.where(lane_head == hh, upd[hh * GLA_DV:(hh + 1) * GLA_DV], 0.0)
        state_ref[...] = new_state


def _gla(gq, gk, gv, glog, gr, layer, gain):
    S = gq.shape[0]
    tb = GLA_BLOCK
    seg, msk = _gla_tables(GLA_CHUNK)
    resident = seg.size * 2 + msk.size * 4
    streamed = tb * (2 * GLA_K_WIDTH * 2 + 2 * GLA_V_WIDTH * 2 + GLA_K_WIDTH * 4 + GLA_V_WIDTH * 2)
    return pl.pallas_call(
        functools.partial(_gla_kernel, tb=tb),
        grid=(S // tb,),
        in_specs=[_rows(tb, GLA_K_WIDTH), _rows(tb, GLA_K_WIDTH), _rows(tb, GLA_V_WIDTH), _rows(tb, GLA_K_WIDTH),
                  _rows(tb, GLA_V_WIDTH), _layer_block(layer, (1, GLA_V_WIDTH)), _resident(seg.shape),
                  _resident(msk.shape)],
        out_specs=_rows(tb, GLA_V_WIDTH),
        out_shape=jax.ShapeDtypeStruct((S, GLA_V_WIDTH), BF16),
        scratch_shapes=[pltpu.VMEM((GLA_DV, GLA_K_WIDTH), F32)],
        compiler_params=pltpu.CompilerParams(dimension_semantics=("arbitrary",),
                                             vmem_limit_bytes=_vmem_limit(resident, streamed)),
        name="gla",
    )(gq, gk, gv, glog, gr, gain, jnp.asarray(seg, BF16), jnp.asarray(msk, F32))


def _mixffn_kernel(*refs, tm):
    n_d = len(DILATIONS)
    o_refs, st_refs = refs[:n_d], refs[n_d:2 * n_d]
    (g_ref, h_ref, expand_ref, again_ref, wa_ref, wg_ref, mpost_ref, fpre_ref, wgate_ref, wup_ref, wdown_ref,
     fpost_ref, out_ref, oslab_ref, stslab_ref) = refs[2 * n_d:]
    a_out = _merge_patterns(o_refs, st_refs, expand_ref, again_ref, oslab_ref, stslab_ref, tm)
    m = _dot(a_out, wa_ref[...]) + _dot(g_ref[...], wg_ref[...])
    h = h_ref[...] + _rms(m, mpost_ref[...])
    x = _rms(h, fpre_ref[...]).astype(BF16)
    acc = None
    for c0 in range(0, D_FF, FF_CHUNK):
        cols = slice(c0, c0 + FF_CHUNK)
        gate = _dot(x, wgate_ref[:, cols])
        act = (gate * jax.nn.sigmoid(gate) * _dot(x, wup_ref[:, cols])).astype(BF16)
        part = _dot(act, wdown_ref[cols, :])
        acc = part if acc is None else acc + part
    out_ref[...] = h + _rms(acc, fpost_ref[...])


def _mixffn(os_, stats, g_out, h, layer, attn_gain, w_out, mix_post, ffn_pre, w_gate, w_up, w_down, ffn_post):
    S = h.shape[0]
    tm = ROW_TILE
    expand = np.zeros((LANES, ATTN_WIDTH), np.float32)
    for hd in range(ATTN_HEADS):
        expand[hd, hd * HEAD_DIM:(hd + 1) * HEAD_DIM] = 1.0
    resident = 2 * (MIX_WIDTH * D_MODEL + 3 * D_MODEL * D_FF) + (ATTN_SLABS + len(DILATIONS)) * tm * LANES * 4
    streamed = tm * (len(DILATIONS) * (ATTN_WIDTH + LANES) * 4 + GLA_V_WIDTH * 2 + 2 * D_MODEL * 4)
    return pl.pallas_call(
        functools.partial(_mixffn_kernel, tm=tm),
        grid=(S // tm,),
        in_specs=([_rows(tm // d, d * ATTN_WIDTH) for d in DILATIONS] + [_rows(tm // d, d * LANES) for d in DILATIONS]
                  + [_rows(tm, GLA_V_WIDTH), _rows(tm, D_MODEL), _resident((LANES, ATTN_WIDTH)),
                     _layer_block(layer, (1, ATTN_WIDTH)),
                     _layer_block(layer, (ATTN_WIDTH, D_MODEL), (0, 0)),
                     _layer_block(layer, (GLA_V_WIDTH, D_MODEL), (ATTN_WIDTH // GLA_V_WIDTH, 0)),
                     _layer_block(layer, (1, D_MODEL)), _layer_block(layer, (1, D_MODEL)),
                     _layer_block(layer, (D_MODEL, D_FF)), _layer_block(layer, (D_MODEL, D_FF)),
                     _layer_block(layer, (D_FF, D_MODEL)), _layer_block(layer, (1, D_MODEL))]),
        out_specs=_rows(tm, D_MODEL),
        out_shape=jax.ShapeDtypeStruct((S, D_MODEL), F32),
        scratch_shapes=[pltpu.VMEM((ATTN_SLABS, tm, LANES), F32), pltpu.VMEM((len(DILATIONS), 1, tm, LANES), F32)],
        compiler_params=pltpu.CompilerParams(dimension_semantics=("arbitrary",),
                                             vmem_limit_bytes=_vmem_limit(resident, streamed)),
        name="mixffn",
    )(*os_, *stats, g_out, h, jnp.asarray(expand, BF16), attn_gain, w_out, w_out, mix_post, ffn_pre,
      w_gate, w_up, w_down, ffn_post)


def _rope_inv_freq():
    inv = np.zeros((1, LANES), np.float32)
    freqs = np.asarray(ROPE_THETA, np.float32) ** (-(np.arange(ROPE_HALF, dtype=np.float32) * 2.0) / ROPE_DIM)
    for lane in range(LANES):
        if lane % HEAD_DIM < ROPE_DIM:
            inv[0, lane] = freqs[lane % ROPE_HALF]
    return jnp.asarray(inv)


def kernel(x, mix_pre_norm, mix_post_norm, ffn_pre_norm, ffn_post_norm, w_in, gla_w_gate_up, gla_b_gate,
           gla_out_norm, attn_out_norm, w_out, w_gate, w_up, w_down):
    B, S, D = x.shape
    depth = mix_pre_norm.shape[0]
    assert D == D_MODEL and S % (max(DILATIONS) * ATTN_SUB) == 0 and S % GLA_BLOCK == 0 and S % ROW_TILE == 0
    assert DILATIONS[0] == 1 and all(b % a == 0 for a, b in zip(DILATIONS, DILATIONS[1:]))
    inv_freq = _rope_inv_freq()
    rope_offsets = _rope_offsets(inv_freq, ROW_TILE)
    gains = lambda t: t.reshape(depth, 1, -1)
    mix_pre, mix_post, ffn_pre, ffn_post = gains(mix_pre_norm), gains(mix_post_norm), gains(ffn_pre_norm), gains(ffn_post_norm)
    gla_norm, attn_norm, b_gate = gains(gla_out_norm), gains(attn_out_norm), gains(gla_b_gate)
    w_in_b = w_in.astype(BF16)
    w_low = jnp.pad(w_in[:, :, MAIN_WIDTH:], ((0, 0), (0, 0), (0, LANES - GLA_GATE_RANK))).astype(BF16)
    w_gu = jnp.pad(gla_w_gate_up, ((0, 0), (0, LANES - GLA_GATE_RANK), (0, 0))).astype(BF16)
    w_out_b, w_gate_b, w_up_b, w_down_b = (t.astype(BF16) for t in (w_out, w_gate, w_up, w_down))

    outs = []
    for b in range(B):
        h = x[b]
        for l in range(depth):
            res = _inproj(h, l, mix_pre, w_in_b, w_low, w_gu, b_gate, inv_freq, rope_offsets)
            n_d = len(DILATIONS)
            qs, ks, vs = res[0:n_d], res[n_d:2 * n_d], res[2 * n_d:3 * n_d]
            gq, gk, gv, gr, glog = res[3 * n_d:]
            pats = [_dilattn(qs[i], ks[i], vs[i], d) for i, d in enumerate(DILATIONS)]
            g_out = _gla(gq, gk, gv, glog, gr, l, gla_norm)
            h = _mixffn([p[0] for p in pats], [p[1] for p in pats], g_out, h, l, attn_norm, w_out_b, mix_post,
                        ffn_pre, w_gate_b, w_up_b, w_down_b, ffn_post)
        outs.append(h)
    return jnp.stack(outs, axis=0)
```

```python
import functools

import numpy as np
import jax
import jax.numpy as jnp
from jax import lax
from jax.experimental import pallas as pl
from jax.experimental.pallas import tpu as pltpu

F32 = jnp.float32
BF16 = jnp.bfloat16

D_MODEL = 1024
HEAD_DIM = 64
ATTN_HEADS = 8
ATTN_WIDTH = ATTN_HEADS * HEAD_DIM
GLA_HEADS = 4
GLA_DK = 64
GLA_DV = 128
GLA_K_WIDTH = GLA_HEADS * GLA_DK
GLA_V_WIDTH = GLA_HEADS * GLA_DV
GLA_GATE_RANK = 16
GLA_TAU = 16.0
MIX_WIDTH = ATTN_WIDTH + GLA_V_WIDTH
D_FF = 2816
ROPE_THETA = 500000.0
ROPE_DIM = HEAD_DIM // 4
ROPE_HALF = ROPE_DIM // 2
DILATIONS = (1, 4, 16)
WINDOW_BACK = 128
RMS_EPS = 1e-6
MAIN_WIDTH = 3 * ATTN_WIDTH + 2 * GLA_K_WIDTH + 2 * GLA_V_WIDTH

LANES = 128
MXU_DIM = 256
VMEM_BYTES_V7X = 64 * 1024 * 1024
VMEM_REQUEST_FLOOR = 52 * 1024 * 1024

ROW_TILE = 512
INPROJ_TILE = 1024
ATTN_Q_TILE = 2048
ATTN_SUB = WINDOW_BACK
HEADS_PER_GROUP = MXU_DIM // HEAD_DIM
ATTN_GROUPS = ATTN_WIDTH // MXU_DIM
ATTN_SLABS = ATTN_WIDTH // LANES
STAT_SUM_LANE = ATTN_HEADS
GLA_CHUNK = 128
GLA_BLOCK = 1024
FF_CHUNK = 256
NEG_BIG = -1e30
MERGE_COEF_PIECES = 1
GLA_DECAY_PIECES = 2
LOG2_E = 1.4426950408889634


def _dot(a, b):
    return jnp.dot(a, b, preferred_element_type=F32)


def _dot_nt(a, b):
    return lax.dot_general(a, b, (((1,), (1,)), ((), ())), preferred_element_type=F32)


def _dot_tn(a, b):
    return lax.dot_general(a, b, (((0,), (0,)), ((), ())), preferred_element_type=F32)


def _rms(x, gain):
    return x * lax.rsqrt(jnp.mean(x * x, axis=-1, keepdims=True) + RMS_EPS) * gain


def _split3(x):
    hi = x.astype(BF16)
    r1 = x - hi.astype(F32)
    mid = r1.astype(BF16)
    lo = (r1 - mid.astype(F32)).astype(BF16)
    return hi, mid, lo


def _vmem_limit(resident_bytes, streamed_bytes):
    need = 2 * (resident_bytes + 2 * streamed_bytes)
    return int(min(max(need, VMEM_REQUEST_FLOOR), VMEM_BYTES_V7X - 8 * 1024 * 1024))


def _resident(shape):
    nd = len(shape)
    return pl.BlockSpec(shape, lambda *_: (0,) * nd, pipeline_mode=pl.Buffered(1))


def _layer_block(layer, shape, index=None):
    index = (0,) * len(shape) if index is None else index
    return pl.BlockSpec((None,) + tuple(shape), lambda *_: (layer,) + tuple(index), pipeline_mode=pl.Buffered(1))


def _rows(tm, width):
    return pl.BlockSpec((tm, width), lambda i: (i, 0))


def _rope_offsets_kernel(inv_ref, rot_ref):
    off = lax.broadcasted_iota(jnp.int32, rot_ref.shape[1:], 0).astype(F32) * inv_ref[...]
    rot_ref[0] = jnp.cos(off)
    rot_ref[1] = jnp.sin(off)


def _rope_offsets(inv_freq, tm):
    return pl.pallas_call(_rope_offsets_kernel, out_shape=jax.ShapeDtypeStruct((2, tm, LANES), F32),
                          name="ropeoffsets")(inv_freq)


def _inproj_kernel(x_ref, gain_ref, w_ref, wlow_ref, wup_ref, bias_ref, inv_ref, rot_ref, *refs, tm):
    qkv_refs = refs[:3 * len(DILATIONS)]
    gq_ref, gk_ref, gv_ref, gr_ref, glog_ref, slab_ref = refs[3 * len(DILATIONS):]
    xb = _rms(x_ref[...], gain_ref[...]).astype(BF16)

    def proj(c0, width):
        return _dot(xb, w_ref[:, c0:c0 + width])

    base = (pl.program_id(0) * tm).astype(F32) * inv_ref[...]
    cos_b, sin_b = jnp.cos(base), jnp.sin(base)
    cos = cos_b * rot_ref[0] - sin_b * rot_ref[1]
    sin = sin_b * rot_ref[0] + cos_b * rot_ref[1]
    lane = lax.broadcasted_iota(jnp.int32, (tm, LANES), 1) % HEAD_DIM
    cosf = jnp.concatenate([cos] * ATTN_SLABS, axis=1)
    sin_first = jnp.concatenate([jnp.where(lane < ROPE_HALF, -sin, 0.0)] * ATTN_SLABS, axis=1)
    sin_second = jnp.concatenate([jnp.where(lane >= ROPE_HALF, sin, 0.0)] * ATTN_SLABS, axis=1)

    def rope(t):
        return (t * cosf + pltpu.roll(t, ATTN_WIDTH - ROPE_HALF, 1) * sin_first
                + pltpu.roll(t, ROPE_HALF, 1) * sin_second)

    def emit_layouts(t, out_refs):
        for s in range(ATTN_SLABS):
            slab_ref[0, s] = t[:, s * LANES:(s + 1) * LANES]
        src, d_prev = 0, 1
        for d, out_ref in zip(DILATIONS, out_refs):
            if d == 1:
                out_ref[...] = t.astype(BF16)
                continue
            n, n_prev, ratio = tm // d, tm // d_prev, d // d_prev
            keep = d != DILATIONS[-1]
            for r in range(d):
                start = (r % d_prev) * n_prev + r // d_prev
                part = jnp.concatenate([slab_ref[src, s, pl.ds(start, n, stride=ratio), :]
                                        for s in range(ATTN_SLABS)], axis=1)
                out_ref[:, r * ATTN_WIDTH:(r + 1) * ATTN_WIDTH] = part.astype(BF16)
                if keep:
                    for s in range(ATTN_SLABS):
                        slab_ref[1 - src, s, r * n:(r + 1) * n, :] = part[:, s * LANES:(s + 1) * LANES]
            src, d_prev = 1 - src, d

    g_low = _dot(xb, wlow_ref[...])
    z = _dot(g_low.astype(BF16), wup_ref[...]) + bias_ref[...]
    log_sig = jnp.minimum(z, 0.0) - jnp.log1p(jnp.exp(-jnp.abs(z)))
    glog_ref[...] = log_sig * (LOG2_E / GLA_TAU)

    n_d = len(DILATIONS)
    c0 = 0
    emit_layouts(rope(proj(c0, ATTN_WIDTH)) * (HEAD_DIM ** -0.5 * LOG2_E), qkv_refs[0:n_d])
    c0 += ATTN_WIDTH
    emit_layouts(rope(proj(c0, ATTN_WIDTH)), qkv_refs[n_d:2 * n_d])
    c0 += ATTN_WIDTH
    emit_layouts(proj(c0, ATTN_WIDTH), qkv_refs[2 * n_d:3 * n_d])
    c0 += ATTN_WIDTH
    gq_ref[...] = (proj(c0, GLA_K_WIDTH) * (GLA_DK ** -0.5)).astype(BF16)
    c0 += GLA_K_WIDTH
    gk_ref[...] = proj(c0, GLA_K_WIDTH).astype(BF16)
    c0 += GLA_K_WIDTH
    gv_ref[...] = proj(c0, GLA_V_WIDTH).astype(BF16)
    c0 += GLA_V_WIDTH
    gr_ref[...] = proj(c0, GLA_V_WIDTH).astype(BF16)


def _inproj(h, layer, gain, w_in, w_low, w_up, bias, inv_freq, rope_offsets):
    S = h.shape[0]
    tm = rope_offsets.shape[1]
    qkv_specs, qkv_shapes = [], []
    for _ in range(3):
        for d in DILATIONS:
            qkv_specs.append(_rows(tm // d, d * ATTN_WIDTH))
            qkv_shapes.append(jax.ShapeDtypeStruct((S // d, d * ATTN_WIDTH), BF16))
    gla_widths = (GLA_K_WIDTH,) * 2 + (GLA_V_WIDTH,) * 2
    out_shape = qkv_shapes + [jax.ShapeDtypeStruct((S, w), BF16) for w in gla_widths] + [jax.ShapeDtypeStruct((S, GLA_K_WIDTH), F32)]
    resident = 2 * (D_MODEL * MAIN_WIDTH + 2 * D_MODEL * LANES)
    streamed = tm * (D_MODEL * 4 + 3 * len(DILATIONS) * ATTN_WIDTH * 2 + MAIN_WIDTH * 2) + tm * MAIN_WIDTH * 4
    return pl.pallas_call(
        functools.partial(_inproj_kernel, tm=tm),
        grid=(S // tm,),
        in_specs=[_rows(tm, D_MODEL), _layer_block(layer, (1, D_MODEL)), _layer_block(layer, (D_MODEL, MAIN_WIDTH)),
                  _layer_block(layer, (D_MODEL, LANES)), _layer_block(layer, (LANES, GLA_K_WIDTH)),
                  _layer_block(layer, (1, GLA_K_WIDTH)), _resident((1, LANES)), _resident((2, tm, LANES))],
        out_specs=qkv_specs + [_rows(tm, w) for w in gla_widths] + [_rows(tm, GLA_K_WIDTH)],
        out_shape=out_shape,
        scratch_shapes=[pltpu.VMEM((2, ATTN_SLABS, tm, LANES), F32)],
        compiler_params=pltpu.CompilerParams(dimension_semantics=("arbitrary",),
                                             vmem_limit_bytes=_vmem_limit(resident, streamed)),
        name="inproj",
    )(h, gain, w_in, w_low, w_up, bias, inv_freq, rope_offsets)


def _dilattn_kernel(q_ref, kc_ref, kp_ref, vc_ref, vp_ref, o_ref, stat_ref, *, tq):
    first_tile = pl.program_id(1) == 0
    sub = ATTN_SUB
    grp = MXU_DIM
    a = lax.broadcasted_iota(jnp.int32, (sub, 2 * sub), 0)
    c = lax.broadcasted_iota(jnp.int32, (sub, 2 * sub), 1)
    band = (c >= a) & (c <= a + WINDOW_BACK)
    band_first = band & ((c >= sub) | jnp.logical_not(first_tile))
    q_head = lax.broadcasted_iota(jnp.int32, (sub, grp), 1) // HEAD_DIM
    stat_lane = lax.broadcasted_iota(jnp.int32, (sub, LANES), 1)

    for j in range(tq // sub):
        rows = slice(j * sub, (j + 1) * sub)
        valid = jnp.concatenate([band_first if j == 0 else band] * HEADS_PER_GROUP, axis=0)
        stats = jnp.zeros((sub, LANES), F32)
        for g in range(ATTN_GROUPS):
            cols = slice(g * grp, (g + 1) * grp)
            qg = q_ref[rows, cols]
            if j == 0:
                kg = jnp.concatenate([kp_ref[:, cols], kc_ref[0:sub, cols]], axis=0)
                vg = jnp.concatenate([vp_ref[:, cols], vc_ref[0:sub, cols]], axis=0)
            else:
                kg = kc_ref[(j - 1) * sub:(j + 1) * sub, cols]
                vg = vc_ref[(j - 1) * sub:(j + 1) * sub, cols]
            qs = jnp.concatenate([jnp.where(q_head == hh, qg, jnp.zeros_like(qg))
                                  for hh in range(HEADS_PER_GROUP)], axis=0)
            s = jnp.where(valid, _dot_nt(qs, kg), NEG_BIG)
            m = jnp.max(s, axis=-1, keepdims=True)
            p = jnp.exp2(s - m)
            sums = jnp.sum(p, axis=-1, keepdims=True)
            pv = _dot(p.astype(BF16), vg)
            o_grp = jnp.zeros((sub, grp), F32)
            for hh in range(HEADS_PER_GROUP):
                hr = slice(hh * sub, (hh + 1) * sub)
                head = g * HEADS_PER_GROUP + hh
                o_grp = jnp.where(q_head == hh, pv[hr], o_grp)
                stats = jnp.where(stat_lane == head, m[hr], stats)
                stats = jnp.where(stat_lane == STAT_SUM_LANE + head, sums[hr], stats)
            o_ref[rows, cols] = o_grp
        stat_ref[rows, :] = stats


def _dilattn(q, k, v, d):
    rows = q.shape[0]
    tq = min(ATTN_Q_TILE, rows)
    W = ATTN_WIDTH
    cur = pl.BlockSpec((tq, W), lambda r, i: (i, r))
    prev = pl.BlockSpec((ATTN_SUB, W), lambda r, i: (jnp.maximum(i * (tq // ATTN_SUB) - 1, 0), r))
    streamed = tq * W * (3 * 2 + 4) + 2 * ATTN_SUB * W * 2 + tq * LANES * 4
    return pl.pallas_call(
        functools.partial(_dilattn_kernel, tq=tq),
        grid=(d, rows // tq),
        in_specs=[cur, cur, prev, cur, prev],
        out_specs=[pl.BlockSpec((tq, W), lambda r, i: (i, r)), pl.BlockSpec((tq, LANES), lambda r, i: (i, r))],
        out_shape=[jax.ShapeDtypeStruct((rows, d * W), F32), jax.ShapeDtypeStruct((rows, d * LANES), F32)],
        compiler_params=pltpu.CompilerParams(dimension_semantics=("arbitrary", "arbitrary"),
                                             vmem_limit_bytes=_vmem_limit(0, streamed)),
        name=f"dilattn{d}",
    )(q, k, k, v, v)


def _merge_patterns(o_refs, st_refs, expand_ref, gain_ref, oslab_ref, stslab_ref, tm):
    def natural_rows(ref, d, width, slab_ref):
        if d == 1:
            return ref[...]
        n = tm // d
        slabs = width // LANES
        for r in range(d):
            for s in range(slabs):
                slab_ref[s, pl.ds(r, n, stride=d), :] = ref[:, r * width + s * LANES:r * width + (s + 1) * LANES]
        return jnp.concatenate([slab_ref[s] for s in range(slabs)], axis=1)

    stats = [natural_rows(st_refs[i], d, LANES, stslab_ref.at[i]) for i, d in enumerate(DILATIONS)]
    lane = lax.broadcasted_iota(jnp.int32, (tm, LANES), 1)
    top = functools.reduce(jnp.maximum, stats)
    scale = [jnp.exp2(st - top) for st in stats]
    total = sum(sc * pltpu.roll(st, LANES - STAT_SUM_LANE, 1) for sc, st in zip(scale, stats))
    inv_total = jnp.where(lane < ATTN_HEADS, 1.0 / total, 0.0)
    acc = None
    for i, d in enumerate(DILATIONS):
        parts = _split3(scale[i] * inv_total)[:MERGE_COEF_PIECES]
        coef = sum(_dot(p, expand_ref[...]) for p in parts)
        term = coef * natural_rows(o_refs[i], d, ATTN_WIDTH, oslab_ref)
        acc = term if acc is None else acc + term
    return _rms(acc, gain_ref[...]).astype(BF16)


def _gla_tables(C):
    levels = int(np.log2(C))
    t = np.arange(C)[:, None]
    j = np.arange(C)[None, :]
    masks = []
    for l in range(levels):
        c = 1 << l
        second = (t % (2 * c)) >= c
        masks.append((t // (2 * c) == j // (2 * c)) & second & ((j % (2 * c)) < c))
    masks.append(t == j)
    seg = (j <= t).astype(np.float32)
    msk = np.stack([np.tile(mm, (GLA_HEADS, 1)) for mm in masks]).astype(np.float32)
    return seg, msk


def _gla_kernel(q_ref, k_ref, v_ref, g_ref, r_ref, gain_ref, seg_ref, msk_ref, o_ref, state_ref, *, tb):
    C = GLA_CHUNK
    levels = msk_ref.shape[0] - 1
    lane_head = lax.broadcasted_iota(jnp.int32, (C, GLA_K_WIDTH), 1) // GLA_DK
    row = lax.broadcasted_iota(jnp.int32, (C, GLA_K_WIDTH), 0)

    @pl.when(pl.program_id(0) == 0)
    def _():
        state_ref[...] = jnp.zeros_like(state_ref)

    def stack_heads(t):
        return jnp.concatenate([jnp.where(lane_head == hh, t, jnp.zeros_like(t)) for hh in range(GLA_HEADS)], axis=0)

    for ci in range(tb // C):
        rows = slice(ci * C, (ci + 1) * C)
        q = q_ref[rows, :].astype(F32)
        k = k_ref[rows, :].astype(F32)
        v = v_ref[rows, :]
        b = sum(_dot(seg_ref[...], part) for part in _split3(g_ref[rows, :])[:GLA_DECAY_PIECES])
        u_from_start = jnp.exp2(jnp.minimum(b, 0.0))
        u_to_end = jnp.exp2(jnp.minimum(b[C - 1:C, :] - b, 0.0))

        scores = msk_ref[levels] * _dot_nt(stack_heads(q.astype(BF16)), k.astype(BF16))
        block_end = b
        for l in range(levels):
            c = 1 << l
            first_half = (row & c) == 0
            b_mid = jnp.where(first_half, block_end, pltpu.roll(block_end, c, 0))
            ul = jnp.exp2(-jnp.abs(b - b_mid))
            scores = scores + msk_ref[l] * _dot_nt(stack_heads((q * ul).astype(BF16)), (k * ul).astype(BF16))
            if l + 1 < levels:
                block_end = jnp.where(first_half, pltpu.roll(block_end, C - c, 0), block_end)

        state_t = state_ref[...]
        inter = _dot_nt(stack_heads((q * u_from_start).astype(BF16)), state_t.astype(BF16))
        scores = scores.astype(BF16)
        gain = gain_ref[...]
        for hh in range(GLA_HEADS):
            hr = slice(hh * C, (hh + 1) * C)
            vc = slice(hh * GLA_DV, (hh + 1) * GLA_DV)
            o_h = _dot(scores[hr], v[:, vc]) + inter[hr]
            o_h = _rms(o_h, gain[:, vc])
            gate = r_ref[rows, vc].astype(F32)
            o_ref[rows, vc] = (o_h * (gate * jax.nn.sigmoid(gate))).astype(BF16)

        upd = _dot_tn(v, (k * u_to_end).astype(BF16))
        new_state = state_t * u_from_start[C - 1:C, :]
        for hh in range(GLA_HEADS):
            new_state = new_state + jnp.where(lane_head == hh, upd[hh * GLA_DV:(hh + 1) * GLA_DV], 0.0)
        state_ref[...] = new_state


def _gla(gq, gk, gv, glog, gr, layer, gain):
    S = gq.shape[0]
    tb = GLA_BLOCK
    seg, msk = _gla_tables(GLA_CHUNK)
    resident = seg.size * 2 + msk.size * 4
    streamed = tb * (2 * GLA_K_WIDTH * 2 + 2 * GLA_V_WIDTH * 2 + GLA_K_WIDTH * 4 + GLA_V_WIDTH * 2)
    return pl.pallas_call(
        functools.partial(_gla_kernel, tb=tb),
        grid=(S // tb,),
        in_specs=[_rows(tb, GLA_K_WIDTH), _rows(tb, GLA_K_WIDTH), _rows(tb, GLA_V_WIDTH), _rows(tb, GLA_K_WIDTH),
                  _rows(tb, GLA_V_WIDTH), _layer_block(layer, (1, GLA_V_WIDTH)), _resident(seg.shape),
                  _resident(msk.shape)],
        out_specs=_rows(tb, GLA_V_WIDTH),
        out_shape=jax.ShapeDtypeStruct((S, GLA_V_WIDTH), BF16),
        scratch_shapes=[pltpu.VMEM((GLA_DV, GLA_K_WIDTH), F32)],
        compiler_params=pltpu.CompilerParams(dimension_semantics=("arbitrary",),
                                             vmem_limit_bytes=_vmem_limit(resident, streamed)),
        name="gla",
    )(gq, gk, gv, glog, gr, gain, jnp.asarray(seg, BF16), jnp.asarray(msk, F32))


def _mixffn_kernel(*refs, tm):
    n_d = len(DILATIONS)
    o_refs, st_refs = refs[:n_d], refs[n_d:2 * n_d]
    (g_ref, h_ref, expand_ref, again_ref, wa_ref, wg_ref, mpost_ref, fpre_ref, wgate_ref, wup_ref, wdown_ref,
     fpost_ref, out_ref, oslab_ref, stslab_ref) = refs[2 * n_d:]
    a_out = _merge_patterns(o_refs, st_refs, expand_ref, again_ref, oslab_ref, stslab_ref, tm)
    m = _dot(a_out, wa_ref[...]) + _dot(g_ref[...], wg_ref[...])
    h = h_ref[...] + _rms(m, mpost_ref[...])
    x = _rms(h, fpre_ref[...]).astype(BF16)
    acc = None
    for c0 in range(0, D_FF, FF_CHUNK):
        cols = slice(c0, c0 + FF_CHUNK)
        gate = _dot(x, wgate_ref[:, cols])
        act = (gate * jax.nn.sigmoid(gate) * _dot(x, wup_ref[:, cols])).astype(BF16)
        part = _dot(act, wdown_ref[cols, :])
        acc = part if acc is None else acc + part
    out_ref[...] = h + _rms(acc, fpost_ref[...])


def _mixffn(os_, stats, g_out, h, layer, attn_gain, w_out, mix_post, ffn_pre, w_gate, w_up, w_down, ffn_post):
    S = h.shape[0]
    tm = ROW_TILE
    expand = np.zeros((LANES, ATTN_WIDTH), np.float32)
    for hd in range(ATTN_HEADS):
        expand[hd, hd * HEAD_DIM:(hd + 1) * HEAD_DIM] = 1.0
    resident = 2 * (MIX_WIDTH * D_MODEL + 3 * D_MODEL * D_FF) + (ATTN_SLABS + len(DILATIONS)) * tm * LANES * 4
    streamed = tm * (len(DILATIONS) * (ATTN_WIDTH + LANES) * 4 + GLA_V_WIDTH * 2 + 2 * D_MODEL * 4)
    return pl.pallas_call(
        functools.partial(_mixffn_kernel, tm=tm),
        grid=(S // tm,),
        in_specs=([_rows(tm // d, d * ATTN_WIDTH) for d in DILATIONS] + [_rows(tm // d, d * LANES) for d in DILATIONS]
                  + [_rows(tm, GLA_V_WIDTH), _rows(tm, D_MODEL), _resident((LANES, ATTN_WIDTH)),
                     _layer_block(layer, (1, ATTN_WIDTH)),
                     _layer_block(layer, (ATTN_WIDTH, D_MODEL), (0, 0)),
                     _layer_block(layer, (GLA_V_WIDTH, D_MODEL), (ATTN_WIDTH // GLA_V_WIDTH, 0)),
                     _layer_block(layer, (1, D_MODEL)), _layer_block(layer, (1, D_MODEL)),
                     _layer_block(layer, (D_MODEL, D_FF)), _layer_block(layer, (D_MODEL, D_FF)),
                     _layer_block(layer, (D_FF, D_MODEL)), _layer_block(layer, (1, D_MODEL))]),
        out_specs=_rows(tm, D_MODEL),
        out_shape=jax.ShapeDtypeStruct((S, D_MODEL), F32),
        scratch_shapes=[pltpu.VMEM((ATTN_SLABS, tm, LANES), F32), pltpu.VMEM((len(DILATIONS), 1, tm, LANES), F32)],
        compiler_params=pltpu.CompilerParams(dimension_semantics=("arbitrary",),
                                             vmem_limit_bytes=_vmem_limit(resident, streamed)),
        name="mixffn",
    )(*os_, *stats, g_out, h, jnp.asarray(expand, BF16), attn_gain, w_out, w_out, mix_post, ffn_pre,
      w_gate, w_up, w_down, ffn_post)


def _rope_inv_freq():
    inv = np.zeros((1, LANES), np.float32)
    freqs = np.asarray(ROPE_THETA, np.float32) ** (-(np.arange(ROPE_HALF, dtype=np.float32) * 2.0) / ROPE_DIM)
    for lane in range(LANES):
        if lane % HEAD_DIM < ROPE_DIM:
            inv[0, lane] = freqs[lane % ROPE_HALF]
    return jnp.asarray(inv)


def kernel(x, mix_pre_norm, mix_post_norm, ffn_pre_norm, ffn_post_norm, w_in, gla_w_gate_up, gla_b_gate,
           gla_out_norm, attn_out_norm, w_out, w_gate, w_up, w_down):
    B, S, D = x.shape
    depth = mix_pre_norm.shape[0]
    assert D == D_MODEL and S % (max(DILATIONS) * ATTN_SUB) == 0
    assert S % GLA_BLOCK == 0 and S % ROW_TILE == 0 and S % INPROJ_TILE == 0
    assert DILATIONS[0] == 1 and all(b % a == 0 for a, b in zip(DILATIONS, DILATIONS[1:]))
    inv_freq = _rope_inv_freq()
    rope_offsets = _rope_offsets(inv_freq, INPROJ_TILE)
    gains = lambda t: t.reshape(depth, 1, -1)
    mix_pre, mix_post, ffn_pre, ffn_post = gains(mix_pre_norm), gains(mix_post_norm), gains(ffn_pre_norm), gains(ffn_post_norm)
    gla_norm, attn_norm, b_gate = gains(gla_out_norm), gains(attn_out_norm), gains(gla_b_gate)
    w_in_b = w_in.astype(BF16)
    w_low = jnp.pad(w_in[:, :, MAIN_WIDTH:], ((0, 0), (0, 0), (0, LANES - GLA_GATE_RANK))).astype(BF16)
    w_gu = jnp.pad(gla_w_gate_up, ((0, 0), (0, LANES - GLA_GATE_RANK), (0, 0))).astype(BF16)
    w_out_b, w_gate_b, w_up_b, w_down_b = (t.astype(BF16) for t in (w_out, w_gate, w_up, w_down))

    outs = []
    for b in range(B):
        h = x[b]
        for l in range(depth):
            res = _inproj(h, l, mix_pre, w_in_b, w_low, w_gu, b_gate, inv_freq, rope_offsets)
            n_d = len(DILATIONS)
            qs, ks, vs = res[0:n_d], res[n_d:2 * n_d], res[2 * n_d:3 * n_d]
            gq, gk, gv, gr, glog = res[3 * n_d:]
            pats = [_dilattn(qs[i], ks[i], vs[i], d) for i, d in enumerate(DILATIONS)]
            g_out = _gla(gq, gk, gv, glog, gr, l, gla_norm)
            h = _mixffn([p[0] for p in pats], [p[1] for p in pats], g_out, h, l, attn_norm, w_out_b, mix_post,
                        ffn_pre, w_gate_b, w_up_b, w_down_b, ffn_post)
        outs.append(h)
    return jnp.stack(outs, axis=0)
```

```python
import functools

import numpy as np
import jax
import jax.numpy as jnp
from jax import lax
from jax.experimental import pallas as pl
from jax.experimental.pallas import tpu as pltpu

F32 = jnp.float32
BF16 = jnp.bfloat16

D_MODEL = 1024
HEAD_DIM = 64
ATTN_HEADS = 8
ATTN_WIDTH = ATTN_HEADS * HEAD_DIM
GLA_HEADS = 4
GLA_DK = 64
GLA_DV = 128
GLA_K_WIDTH = GLA_HEADS * GLA_DK
GLA_V_WIDTH = GLA_HEADS * GLA_DV
GLA_GATE_RANK = 16
GLA_TAU = 16.0
MIX_WIDTH = ATTN_WIDTH + GLA_V_WIDTH
D_FF = 2816
ROPE_THETA = 500000.0
ROPE_DIM = HEAD_DIM // 4
ROPE_HALF = ROPE_DIM // 2
DILATIONS = (1, 4, 16)
WINDOW_BACK = 128
RMS_EPS = 1e-6
MAIN_WIDTH = 3 * ATTN_WIDTH + 2 * GLA_K_WIDTH + 2 * GLA_V_WIDTH

LANES = 128
MXU_DIM = 256
VMEM_BYTES_V7X = 64 * 1024 * 1024
VMEM_REQUEST_FLOOR = 52 * 1024 * 1024

ROW_TILE = 512
INPROJ_TILE = 1024
ATTN_Q_TILE = 2048
ATTN_SUB = WINDOW_BACK
HEADS_PER_GROUP = MXU_DIM // HEAD_DIM
ATTN_GROUPS = ATTN_WIDTH // MXU_DIM
ATTN_SLABS = ATTN_WIDTH // LANES
STAT_SUM_LANE = ATTN_HEADS
GLA_CHUNK = 128
GLA_BLOCK = 1024
GLA_ROW_SPLIT_MIN = 8
FF_CHUNK = 256
NEG_BIG = -1e30
MERGE_COEF_PIECES = 1
GLA_DECAY_PIECES = 2
LOG2_E = 1.4426950408889634


def _dot(a, b):
    return jnp.dot(a, b, preferred_element_type=F32)


def _dot_nt(a, b):
    return lax.dot_general(a, b, (((1,), (1,)), ((), ())), preferred_element_type=F32)


def _dot_tn(a, b):
    return lax.dot_general(a, b, (((0,), (0,)), ((), ())), preferred_element_type=F32)


def _rms(x, gain):
    return x * lax.rsqrt(jnp.mean(x * x, axis=-1, keepdims=True) + RMS_EPS) * gain


def _split3(x):
    hi = x.astype(BF16)
    r1 = x - hi.astype(F32)
    mid = r1.astype(BF16)
    lo = (r1 - mid.astype(F32)).astype(BF16)
    return hi, mid, lo


def _vmem_limit(resident_bytes, streamed_bytes):
    need = 2 * (resident_bytes + 2 * streamed_bytes)
    return int(min(max(need, VMEM_REQUEST_FLOOR), VMEM_BYTES_V7X - 8 * 1024 * 1024))


def _resident(shape):
    nd = len(shape)
    return pl.BlockSpec(shape, lambda *_: (0,) * nd, pipeline_mode=pl.Buffered(1))


def _layer_block(layer, shape, index=None):
    index = (0,) * len(shape) if index is None else index
    return pl.BlockSpec((None,) + tuple(shape), lambda *_: (layer,) + tuple(index), pipeline_mode=pl.Buffered(1))


def _rows(tm, width):
    return pl.BlockSpec((tm, width), lambda i: (i, 0))


def _rope_offsets_kernel(inv_ref, rot_ref):
    off = lax.broadcasted_iota(jnp.int32, rot_ref.shape[1:], 0).astype(F32) * inv_ref[...]
    rot_ref[0] = jnp.cos(off)
    rot_ref[1] = jnp.sin(off)


def _rope_offsets(inv_freq, tm):
    return pl.pallas_call(_rope_offsets_kernel, out_shape=jax.ShapeDtypeStruct((2, tm, LANES), F32),
                          name="ropeoffsets")(inv_freq)


def _inproj_kernel(x_ref, gain_ref, w_ref, wlow_ref, wup_ref, bias_ref, inv_ref, rot_ref, *refs, tm):
    qkv_refs = refs[:3 * len(DILATIONS)]
    gq_ref, gk_ref, gv_ref, gr_ref, glog_ref, slab_ref = refs[3 * len(DILATIONS):]
    xb = _rms(x_ref[...], gain_ref[...]).astype(BF16)

    def proj(c0, width):
        return _dot(xb, w_ref[:, c0:c0 + width])

    base = (pl.program_id(0) * tm).astype(F32) * inv_ref[...]
    cos_b, sin_b = jnp.cos(base), jnp.sin(base)
    cos = cos_b * rot_ref[0] - sin_b * rot_ref[1]
    sin = sin_b * rot_ref[0] + cos_b * rot_ref[1]
    lane = lax.broadcasted_iota(jnp.int32, (tm, LANES), 1) % HEAD_DIM
    cosf = jnp.concatenate([cos] * ATTN_SLABS, axis=1)
    sin_first = jnp.concatenate([jnp.where(lane < ROPE_HALF, -sin, 0.0)] * ATTN_SLABS, axis=1)
    sin_second = jnp.concatenate([jnp.where(lane >= ROPE_HALF, sin, 0.0)] * ATTN_SLABS, axis=1)

    def rope(t):
        return (t * cosf + pltpu.roll(t, ATTN_WIDTH - ROPE_HALF, 1) * sin_first
                + pltpu.roll(t, ROPE_HALF, 1) * sin_second)

    def emit_layouts(t, out_refs):
        for s in range(ATTN_SLABS):
            slab_ref[0, s] = t[:, s * LANES:(s + 1) * LANES]
        src, d_prev = 0, 1
        for d, out_ref in zip(DILATIONS, out_refs):
            if d == 1:
                out_ref[...] = t.astype(BF16)
                continue
            n, n_prev, ratio = tm // d, tm // d_prev, d // d_prev
            keep = d != DILATIONS[-1]
            for r in range(d):
                start = (r % d_prev) * n_prev + r // d_prev
                part = jnp.concatenate([slab_ref[src, s, pl.ds(start, n, stride=ratio), :]
                                        for s in range(ATTN_SLABS)], axis=1)
                out_ref[:, r * ATTN_WIDTH:(r + 1) * ATTN_WIDTH] = part.astype(BF16)
                if keep:
                    for s in range(ATTN_SLABS):
                        slab_ref[1 - src, s, r * n:(r + 1) * n, :] = part[:, s * LANES:(s + 1) * LANES]
            src, d_prev = 1 - src, d

    g_low = _dot(xb, wlow_ref[...])
    z = _dot(g_low.astype(BF16), wup_ref[...]) + bias_ref[...]
    log_sig = jnp.minimum(z, 0.0) - jnp.log1p(jnp.exp(-jnp.abs(z)))
    glog_ref[...] = log_sig * (LOG2_E / GLA_TAU)

    n_d = len(DILATIONS)
    c0 = 0
    emit_layouts(rope(proj(c0, ATTN_WIDTH)) * (HEAD_DIM ** -0.5 * LOG2_E), qkv_refs[0:n_d])
    c0 += ATTN_WIDTH
    emit_layouts(rope(proj(c0, ATTN_WIDTH)), qkv_refs[n_d:2 * n_d])
    c0 += ATTN_WIDTH
    emit_layouts(proj(c0, ATTN_WIDTH), qkv_refs[2 * n_d:3 * n_d])
    c0 += ATTN_WIDTH
    gq_ref[...] = (proj(c0, GLA_K_WIDTH) * (GLA_DK ** -0.5)).astype(BF16)
    c0 += GLA_K_WIDTH
    gk_ref[...] = proj(c0, GLA_K_WIDTH).astype(BF16)
    c0 += GLA_K_WIDTH
    gv_ref[...] = proj(c0, GLA_V_WIDTH).astype(BF16)
    c0 += GLA_V_WIDTH
    gr_ref[...] = proj(c0, GLA_V_WIDTH).astype(BF16)


def _inproj(h, layer, gain, w_in, w_low, w_up, bias, inv_freq, rope_offsets):
    S = h.shape[0]
    tm = rope_offsets.shape[1]
    qkv_specs, qkv_shapes = [], []
    for _ in range(3):
        for d in DILATIONS:
            qkv_specs.append(_rows(tm // d, d * ATTN_WIDTH))
            qkv_shapes.append(jax.ShapeDtypeStruct((S // d, d * ATTN_WIDTH), BF16))
    gla_widths = (GLA_K_WIDTH,) * 2 + (GLA_V_WIDTH,) * 2
    out_shape = qkv_shapes + [jax.ShapeDtypeStruct((S, w), BF16) for w in gla_widths] + [jax.ShapeDtypeStruct((S, GLA_K_WIDTH), F32)]
    resident = 2 * (D_MODEL * MAIN_WIDTH + 2 * D_MODEL * LANES)
    streamed = tm * (D_MODEL * 4 + 3 * len(DILATIONS) * ATTN_WIDTH * 2 + MAIN_WIDTH * 2) + tm * MAIN_WIDTH * 4
    return pl.pallas_call(
        functools.partial(_inproj_kernel, tm=tm),
        grid=(S // tm,),
        in_specs=[_rows(tm, D_MODEL), _layer_block(layer, (1, D_MODEL)), _layer_block(layer, (D_MODEL, MAIN_WIDTH)),
                  _layer_block(layer, (D_MODEL, LANES)), _layer_block(layer, (LANES, GLA_K_WIDTH)),
                  _layer_block(layer, (1, GLA_K_WIDTH)), _resident((1, LANES)), _resident((2, tm, LANES))],
        out_specs=qkv_specs + [_rows(tm, w) for w in gla_widths] + [_rows(tm, GLA_K_WIDTH)],
        out_shape=out_shape,
        scratch_shapes=[pltpu.VMEM((2, ATTN_SLABS, tm, LANES), F32)],
        compiler_params=pltpu.CompilerParams(dimension_semantics=("arbitrary",),
                                             vmem_limit_bytes=_vmem_limit(resident, streamed)),
        name="inproj",
    )(h, gain, w_in, w_low, w_up, bias, inv_freq, rope_offsets)


def _dilattn_kernel(q_ref, kc_ref, kp_ref, vc_ref, vp_ref, o_ref, stat_ref, *, tq):
    first_tile = pl.program_id(1) == 0
    sub = ATTN_SUB
    grp = MXU_DIM
    a = lax.broadcasted_iota(jnp.int32, (sub, 2 * sub), 0)
    c = lax.broadcasted_iota(jnp.int32, (sub, 2 * sub), 1)
    band = (c >= a) & (c <= a + WINDOW_BACK)
    band_first = band & ((c >= sub) | jnp.logical_not(first_tile))
    q_head = lax.broadcasted_iota(jnp.int32, (sub, grp), 1) // HEAD_DIM

    for j in range(tq // sub):
        rows = slice(j * sub, (j + 1) * sub)
        valid = jnp.concatenate([band_first if j == 0 else band] * HEADS_PER_GROUP, axis=0)
        for g in range(ATTN_GROUPS):
            cols = slice(g * grp, (g + 1) * grp)
            qg = q_ref[rows, cols]
            if j == 0:
                kg = jnp.concatenate([kp_ref[:, cols], kc_ref[0:sub, cols]], axis=0)
                vg = jnp.concatenate([vp_ref[:, cols], vc_ref[0:sub, cols]], axis=0)
            else:
                kg = kc_ref[(j - 1) * sub:(j + 1) * sub, cols]
                vg = vc_ref[(j - 1) * sub:(j + 1) * sub, cols]
            qs = jnp.concatenate([jnp.where(q_head == hh, qg, jnp.zeros_like(qg))
                                  for hh in range(HEADS_PER_GROUP)], axis=0)
            s = jnp.where(valid, _dot_nt(qs, kg), NEG_BIG)
            m = jnp.max(s, axis=-1, keepdims=True)
            p = jnp.exp2(s - m)
            sums = jnp.sum(p, axis=-1, keepdims=True)
            pv = _dot(p.astype(BF16), vg)
            for hh in range(HEADS_PER_GROUP):
                hr = slice(hh * sub, (hh + 1) * sub)
                head = g * HEADS_PER_GROUP + hh
                lanes = slice(hh * HEAD_DIM, (hh + 1) * HEAD_DIM)
                o_ref[rows, g * grp + hh * HEAD_DIM:g * grp + (hh + 1) * HEAD_DIM] = pv[hr, lanes]
                stat_ref[rows, head:head + 1] = m[hr]
                stat_ref[rows, STAT_SUM_LANE + head:STAT_SUM_LANE + head + 1] = sums[hr]
        stat_ref[rows, 2 * ATTN_HEADS:] = jnp.zeros((sub, LANES - 2 * ATTN_HEADS), F32)


def _dilattn(q, k, v, d):
    rows = q.shape[0]
    tq = min(ATTN_Q_TILE, rows)
    W = ATTN_WIDTH
    cur = pl.BlockSpec((tq, W), lambda r, i: (i, r))
    prev = pl.BlockSpec((ATTN_SUB, W), lambda r, i: (jnp.maximum(i * (tq // ATTN_SUB) - 1, 0), r))
    streamed = tq * W * (3 * 2 + 4) + 2 * ATTN_SUB * W * 2 + tq * LANES * 4
    return pl.pallas_call(
        functools.partial(_dilattn_kernel, tq=tq),
        grid=(d, rows // tq),
        in_specs=[cur, cur, prev, cur, prev],
        out_specs=[pl.BlockSpec((tq, W), lambda r, i: (i, r)), pl.BlockSpec((tq, LANES), lambda r, i: (i, r))],
        out_shape=[jax.ShapeDtypeStruct((rows, d * W), F32), jax.ShapeDtypeStruct((rows, d * LANES), F32)],
        compiler_params=pltpu.CompilerParams(dimension_semantics=("arbitrary", "arbitrary"),
                                             vmem_limit_bytes=_vmem_limit(0, streamed)),
        name=f"dilattn{d}",
    )(q, k, k, v, v)


def _merge_patterns(o_refs, st_refs, expand_ref, gain_ref, oslab_ref, stslab_ref, tm):
    def natural_rows(ref, d, width, slab_ref):
        if d == 1:
            return ref[...]
        n = tm // d
        slabs = width // LANES
        for r in range(d):
            for s in range(slabs):
                slab_ref[s, pl.ds(r, n, stride=d), :] = ref[:, r * width + s * LANES:r * width + (s + 1) * LANES]
        return jnp.concatenate([slab_ref[s] for s in range(slabs)], axis=1)

    stats = [natural_rows(st_refs[i], d, LANES, stslab_ref.at[i]) for i, d in enumerate(DILATIONS)]
    lane = lax.broadcasted_iota(jnp.int32, (tm, LANES), 1)
    top = functools.reduce(jnp.maximum, stats)
    scale = [jnp.exp2(st - top) for st in stats]
    total = sum(sc * pltpu.roll(st, LANES - STAT_SUM_LANE, 1) for sc, st in zip(scale, stats))
    inv_total = jnp.where(lane < ATTN_HEADS, 1.0 / total, 0.0)
    acc = None
    for i, d in enumerate(DILATIONS):
        parts = _split3(scale[i] * inv_total)[:MERGE_COEF_PIECES]
        coef = sum(_dot(p, expand_ref[...]) for p in parts)
        term = coef * natural_rows(o_refs[i], d, ATTN_WIDTH, oslab_ref)
        acc = term if acc is None else acc + term
    return _rms(acc, gain_ref[...]).astype(BF16)


def _gla_tables(C):
    levels = int(np.log2(C))
    t = np.arange(C)[:, None]
    j = np.arange(C)[None, :]
    full, compact = [], []
    for l in range(levels):
        c = 1 << l
        second = (t % (2 * c)) >= c
        mask = (t // (2 * c) == j // (2 * c)) & second & ((j % (2 * c)) < c)
        if c < GLA_ROW_SPLIT_MIN:
            full.append(mask)
        else:
            compact.append(mask[second[:, 0]])
    full.append(t == j)
    seg = (j <= t).astype(np.float32)
    tiled = lambda ms: np.stack([np.tile(mm, (GLA_HEADS, 1)) for mm in ms]).astype(np.float32)
    return seg, tiled(full), tiled(compact)


def _second_half_rows(C, c):
    return [(start, c) for start in range(c, C, 2 * c)]


def _gla_kernel(q_ref, k_ref, v_ref, g_ref, r_ref, gain_ref, seg_ref, msk_ref, mskq_ref, o_ref, state_ref, *, tb):
    C = GLA_CHUNK
    levels = msk_ref.shape[0] - 1 + mskq_ref.shape[0]
    row = lax.broadcasted_iota(jnp.int32, (C, GLA_K_WIDTH), 0)

    @pl.when(pl.program_id(0) == 0)
    def _():
        state_ref[...] = jnp.zeros_like(state_ref)

    def stack_heads(t):
        keep = lax.broadcasted_iota(jnp.int32, t.shape, 1) // GLA_DK
        return jnp.concatenate([jnp.where(keep == hh, t, jnp.zeros_like(t)) for hh in range(GLA_HEADS)], axis=0)

    def add_rows(scores, upd, ranges):
        n = sum(size for _, size in ranges)
        pieces = []
        for hh in range(GLA_HEADS):
            pos, off = 0, 0
            for start, size in ranges:
                if start > pos:
                    pieces.append(scores[hh * C + pos:hh * C + start])
                pieces.append(scores[hh * C + start:hh * C + start + size] + upd[hh * n + off:hh * n + off + size])
                pos, off = start + size, off + size
            if pos < C:
                pieces.append(scores[hh * C + pos:(hh + 1) * C])
        return jnp.concatenate(pieces, axis=0)

    for ci in range(tb // C):
        rows = slice(ci * C, (ci + 1) * C)
        q = q_ref[rows, :].astype(F32)
        k = k_ref[rows, :].astype(F32)
        v = v_ref[rows, :]
        b = sum(_dot(seg_ref[...], part) for part in _split3(g_ref[rows, :])[:GLA_DECAY_PIECES])
        u_from_start = jnp.exp2(jnp.minimum(b, 0.0))
        u_to_end = jnp.exp2(jnp.minimum(b[C - 1:C, :] - b, 0.0))

        n_small = msk_ref.shape[0] - 1
        scores = msk_ref[n_small] * _dot_nt(stack_heads(q.astype(BF16)), k.astype(BF16))
        block_end = b
        for l in range(levels):
            c = 1 << l
            first_half = (row & c) == 0
            b_mid = jnp.where(first_half, block_end, pltpu.roll(block_end, c, 0))
            ul = jnp.exp2(-jnp.abs(b - b_mid))
            keys = (k * ul).astype(BF16)
            if l < n_small:
                scores = scores + msk_ref[l] * _dot_nt(stack_heads((q * ul).astype(BF16)), keys)
            else:
                ranges = _second_half_rows(C, c)
                queries = jnp.concatenate([q[st:st + n] * ul[st:st + n] for st, n in ranges], axis=0)
                upd = mskq_ref[l - n_small] * _dot_nt(stack_heads(queries.astype(BF16)), keys)
                scores = add_rows(scores, upd, ranges)
            if l + 1 < levels:
                block_end = jnp.where(first_half, pltpu.roll(block_end, C - c, 0), block_end)

        state_t = state_ref[...]
        inter = _dot_nt(stack_heads((q * u_from_start).astype(BF16)), state_t.astype(BF16))
        scores = scores.astype(BF16)
        gain = gain_ref[...]
        for hh in range(GLA_HEADS):
            hr = slice(hh * C, (hh + 1) * C)
            vc = slice(hh * GLA_DV, (hh + 1) * GLA_DV)
            o_h = _dot(scores[hr], v[:, vc]) + inter[hr]
            o_h = _rms(o_h, gain[:, vc])
            gate = r_ref[rows, vc].astype(F32)
            o_ref[rows, vc] = (o_h * (gate * jax.nn.sigmoid(gate))).astype(BF16)

        upd = _dot_tn(v, (k * u_to_end).astype(BF16))
        decayed = state_t * u_from_start[C - 1:C, :]
        for hh in range(GLA_HEADS):
            lanes = slice(hh * GLA_DK, (hh + 1) * GLA_DK)
            state_ref[:, lanes] = decayed[:, lanes] + upd[hh * GLA_DV:(hh + 1) * GLA_DV, lanes]


def _gla(gq, gk, gv, glog, gr, layer, gain):
    S = gq.shape[0]
    tb = GLA_BLOCK
    seg, msk, mskq = _gla_tables(GLA_CHUNK)
    resident = seg.size * 2 + (msk.size + mskq.size) * 4
    streamed = tb * (2 * GLA_K_WIDTH * 2 + 2 * GLA_V_WIDTH * 2 + GLA_K_WIDTH * 4 + GLA_V_WIDTH * 2)
    return pl.pallas_call(
        functools.partial(_gla_kernel, tb=tb),
        grid=(S // tb,),
        in_specs=[_rows(tb, GLA_K_WIDTH), _rows(tb, GLA_K_WIDTH), _rows(tb, GLA_V_WIDTH), _rows(tb, GLA_K_WIDTH),
                  _rows(tb, GLA_V_WIDTH), _layer_block(layer, (1, GLA_V_WIDTH)), _resident(seg.shape),
                  _resident(msk.shape), _resident(mskq.shape)],
        out_specs=_rows(tb, GLA_V_WIDTH),
        out_shape=jax.ShapeDtypeStruct((S, GLA_V_WIDTH), BF16),
        scratch_shapes=[pltpu.VMEM((GLA_DV, GLA_K_WIDTH), F32)],
        compiler_params=pltpu.CompilerParams(dimension_semantics=("arbitrary",),
                                             vmem_limit_bytes=_vmem_limit(resident, streamed)),
        name="gla",
    )(gq, gk, gv, glog, gr, gain, jnp.asarray(seg, BF16), jnp.asarray(msk, F32), jnp.asarray(mskq, F32))


def _mixffn_kernel(*refs, tm):
    n_d = len(DILATIONS)
    o_refs, st_refs = refs[:n_d], refs[n_d:2 * n_d]
    (g_ref, h_ref, expand_ref, again_ref, wa_ref, wg_ref, mpost_ref, fpre_ref, wgate_ref, wup_ref, wdown_ref,
     fpost_ref, out_ref, oslab_ref, stslab_ref) = refs[2 * n_d:]
    a_out = _merge_patterns(o_refs, st_refs, expand_ref, again_ref, oslab_ref, stslab_ref, tm)
    m = _dot(a_out, wa_ref[...]) + _dot(g_ref[...], wg_ref[...])
    h = h_ref[...] + _rms(m, mpost_ref[...])
    x = _rms(h, fpre_ref[...]).astype(BF16)
    acc = None
    for c0 in range(0, D_FF, FF_CHUNK):
        cols = slice(c0, c0 + FF_CHUNK)
        gate = _dot(x, wgate_ref[:, cols])
        act = (gate * jax.nn.sigmoid(gate) * _dot(x, wup_ref[:, cols])).astype(BF16)
        part = _dot(act, wdown_ref[cols, :])
        acc = part if acc is None else acc + part
    out_ref[...] = h + _rms(acc, fpost_ref[...])


def _mixffn(os_, stats, g_out, h, layer, attn_gain, w_out, mix_post, ffn_pre, w_gate, w_up, w_down, ffn_post):
    S = h.shape[0]
    tm = ROW_TILE
    expand = np.zeros((LANES, ATTN_WIDTH), np.float32)
    for hd in range(ATTN_HEADS):
        expand[hd, hd * HEAD_DIM:(hd + 1) * HEAD_DIM] = 1.0
    resident = 2 * (MIX_WIDTH * D_MODEL + 3 * D_MODEL * D_FF) + (ATTN_SLABS + len(DILATIONS)) * tm * LANES * 4
    streamed = tm * (len(DILATIONS) * (ATTN_WIDTH + LANES) * 4 + GLA_V_WIDTH * 2 + 2 * D_MODEL * 4)
    return pl.pallas_call(
        functools.partial(_mixffn_kernel, tm=tm),
        grid=(S // tm,),
        in_specs=([_rows(tm // d, d * ATTN_WIDTH) for d in DILATIONS] + [_rows(tm // d, d * LANES) for d in DILATIONS]
                  + [_rows(tm, GLA_V_WIDTH), _rows(tm, D_MODEL), _resident((LANES, ATTN_WIDTH)),
                     _layer_block(layer, (1, ATTN_WIDTH)),
                     _layer_block(layer, (ATTN_WIDTH, D_MODEL), (0, 0)),
                     _layer_block(layer, (GLA_V_WIDTH, D_MODEL), (ATTN_WIDTH // GLA_V_WIDTH, 0)),
                     _layer_block(layer, (1, D_MODEL)), _layer_block(layer, (1, D_MODEL)),
                     _layer_block(layer, (D_MODEL, D_FF)), _layer_block(layer, (D_MODEL, D_FF)),
                     _layer_block(layer, (D_FF, D_MODEL)), _layer_block(layer, (1, D_MODEL))]),
        out_specs=_rows(tm, D_MODEL),
        out_shape=jax.ShapeDtypeStruct((S, D_MODEL), F32),
        scratch_shapes=[pltpu.VMEM((ATTN_SLABS, tm, LANES), F32), pltpu.VMEM((len(DILATIONS), 1, tm, LANES), F32)],
        compiler_params=pltpu.CompilerParams(dimension_semantics=("arbitrary",),
                                             vmem_limit_bytes=_vmem_limit(resident, streamed)),
        name="mixffn",
    )(*os_, *stats, g_out, h, jnp.asarray(expand, BF16), attn_gain, w_out, w_out, mix_post, ffn_pre,
      w_gate, w_up, w_down, ffn_post)


def _rope_inv_freq():
    inv = np.zeros((1, LANES), np.float32)
    freqs = np.asarray(ROPE_THETA, np.float32) ** (-(np.arange(ROPE_HALF, dtype=np.float32) * 2.0) / ROPE_DIM)
    for lane in range(LANES):
        if lane % HEAD_DIM < ROPE_DIM:
            inv[0, lane] = freqs[lane % ROPE_HALF]
    return jnp.asarray(inv)


def kernel(x, mix_pre_norm, mix_post_norm, ffn_pre_norm, ffn_post_norm, w_in, gla_w_gate_up, gla_b_gate,
           gla_out_norm, attn_out_norm, w_out, w_gate, w_up, w_down):
    B, S, D = x.shape
    depth = mix_pre_norm.shape[0]
    assert D == D_MODEL and S % (max(DILATIONS) * ATTN_SUB) == 0
    assert S % GLA_BLOCK == 0 and S % ROW_TILE == 0 and S % INPROJ_TILE == 0
    assert DILATIONS[0] == 1 and all(b % a == 0 for a, b in zip(DILATIONS, DILATIONS[1:]))
    inv_freq = _rope_inv_freq()
    rope_offsets = _rope_offsets(inv_freq, INPROJ_TILE)
    gains = lambda t: t.reshape(depth, 1, -1)
    mix_pre, mix_post, ffn_pre, ffn_post = gains(mix_pre_norm), gains(mix_post_norm), gains(ffn_pre_norm), gains(ffn_post_norm)
    gla_norm, attn_norm, b_gate = gains(gla_out_norm), gains(attn_out_norm), gains(gla_b_gate)
    w_in_b = w_in.astype(BF16)
    w_low = jnp.pad(w_in[:, :, MAIN_WIDTH:], ((0, 0), (0, 0), (0, LANES - GLA_GATE_RANK))).astype(BF16)
    w_gu = jnp.pad(gla_w_gate_up, ((0, 0), (0, LANES - GLA_GATE_RANK), (0, 0))).astype(BF16)
    w_out_b, w_gate_b, w_up_b, w_down_b = (t.astype(BF16) for t in (w_out, w_gate, w_up, w_down))

    outs = []
    for b in range(B):
        h = x[b]
        for l in range(depth):
            res = _inproj(h, l, mix_pre, w_in_b, w_low, w_gu, b_gate, inv_freq, rope_offsets)
            n_d = len(DILATIONS)
            qs, ks, vs = res[0:n_d], res[n_d:2 * n_d], res[2 * n_d:3 * n_d]
            gq, gk, gv, gr, glog = res[3 * n_d:]
            pats = [_dilattn(qs[i], ks[i], vs[i], d) for i, d in enumerate(DILATIONS)]
            g_out = _gla(gq, gk, gv, glog, gr, l, gla_norm)
            h = _mixffn([p[0] for p in pats], [p[1] for p in pats], g_out, h, l, attn_norm, w_out_b, mix_post,
                        ffn_pre, w_gate_b, w_up_b, w_down_b, ffn_post)
        outs.append(h)
    return jnp.stack(outs, axis=0)
```

```python
import functools

import numpy as np
import jax
import jax.numpy as jnp
from jax import lax
from jax.experimental import pallas as pl
from jax.experimental.pallas import tpu as pltpu

F32 = jnp.float32
BF16 = jnp.bfloat16

D_MODEL = 1024
HEAD_DIM = 64
ATTN_HEADS = 8
ATTN_WIDTH = ATTN_HEADS * HEAD_DIM
GLA_HEADS = 4
GLA_DK = 64
GLA_DV = 128
GLA_K_WIDTH = GLA_HEADS * GLA_DK
GLA_V_WIDTH = GLA_HEADS * GLA_DV
GLA_GATE_RANK = 16
GLA_TAU = 16.0
MIX_WIDTH = ATTN_WIDTH + GLA_V_WIDTH
D_FF = 2816
ROPE_THETA = 500000.0
ROPE_DIM = HEAD_DIM // 4
ROPE_HALF = ROPE_DIM // 2
DILATIONS = (1, 4, 16)
WINDOW_BACK = 128
RMS_EPS = 1e-6
MAIN_WIDTH = 3 * ATTN_WIDTH + 2 * GLA_K_WIDTH + 2 * GLA_V_WIDTH

LANES = 128
MXU_DIM = 256
VMEM_BYTES_V7X = 64 * 1024 * 1024
VMEM_REQUEST_FLOOR = 52 * 1024 * 1024

ROW_TILE = 512
INPROJ_TILE = 1024
ATTN_Q_TILE = 2048
ATTN_SUB = WINDOW_BACK
HEADS_PER_GROUP = MXU_DIM // HEAD_DIM
ATTN_GROUPS = ATTN_WIDTH // MXU_DIM
ATTN_SLABS = ATTN_WIDTH // LANES
STAT_SUM_LANE = ATTN_HEADS
GLA_CHUNK = 128
GLA_BLOCK = 1024
GLA_ROW_SPLIT_MIN = 8
FF_CHUNK = 256
NEG_BIG = -1e30
MERGE_COEF_PIECES = 1
GLA_DECAY_PIECES = 2
LOG2_E = 1.4426950408889634


def _dot(a, b):
    return jnp.dot(a, b, preferred_element_type=F32)


def _dot_nt(a, b):
    return lax.dot_general(a, b, (((1,), (1,)), ((), ())), preferred_element_type=F32)


def _dot_tn(a, b):
    return lax.dot_general(a, b, (((0,), (0,)), ((), ())), preferred_element_type=F32)


def _rms(x, gain):
    return x * lax.rsqrt(jnp.mean(x * x, axis=-1, keepdims=True) + RMS_EPS) * gain


def _split3(x):
    hi = x.astype(BF16)
    r1 = x - hi.astype(F32)
    mid = r1.astype(BF16)
    lo = (r1 - mid.astype(F32)).astype(BF16)
    return hi, mid, lo


def _vmem_limit(resident_bytes, streamed_bytes):
    need = 2 * (resident_bytes + 2 * streamed_bytes)
    return int(min(max(need, VMEM_REQUEST_FLOOR), VMEM_BYTES_V7X - 8 * 1024 * 1024))


def _resident(shape):
    nd = len(shape)
    return pl.BlockSpec(shape, lambda *_: (0,) * nd, pipeline_mode=pl.Buffered(1))


def _layer_block(layer, shape, index=None):
    index = (0,) * len(shape) if index is None else index
    return pl.BlockSpec((None,) + tuple(shape), lambda *_: (layer,) + tuple(index), pipeline_mode=pl.Buffered(1))


def _rows(tm, width):
    return pl.BlockSpec((tm, width), lambda i: (i, 0))


def _rope_offsets_kernel(inv_ref, rot_ref):
    off = lax.broadcasted_iota(jnp.int32, rot_ref.shape[1:], 0).astype(F32) * inv_ref[...]
    rot_ref[0] = jnp.cos(off)
    rot_ref[1] = jnp.sin(off)


def _rope_offsets(inv_freq, tm):
    return pl.pallas_call(_rope_offsets_kernel, out_shape=jax.ShapeDtypeStruct((2, tm, LANES), F32),
                          name="ropeoffsets")(inv_freq)


def _inproj_kernel(x_ref, gain_ref, w_ref, wlow_ref, wup_ref, bias_ref, inv_ref, rot_ref, *refs, tm):
    qkv_refs = refs[:3 * len(DILATIONS)]
    gq_ref, gk_ref, gv_ref, gr_ref, glog_ref, slab_ref = refs[3 * len(DILATIONS):]
    xb = _rms(x_ref[...], gain_ref[...]).astype(BF16)

    def proj(c0, width):
        return _dot(xb, w_ref[:, c0:c0 + width])

    base = (pl.program_id(0) * tm).astype(F32) * inv_ref[...]
    cos_b, sin_b = jnp.cos(base), jnp.sin(base)
    cos = cos_b * rot_ref[0] - sin_b * rot_ref[1]
    sin = sin_b * rot_ref[0] + cos_b * rot_ref[1]
    lane = lax.broadcasted_iota(jnp.int32, (tm, LANES), 1) % HEAD_DIM
    cosf = jnp.concatenate([cos] * ATTN_SLABS, axis=1)
    sin_first = jnp.concatenate([jnp.where(lane < ROPE_HALF, -sin, 0.0)] * ATTN_SLABS, axis=1)
    sin_second = jnp.concatenate([jnp.where(lane >= ROPE_HALF, sin, 0.0)] * ATTN_SLABS, axis=1)

    def rope(t):
        return (t * cosf + pltpu.roll(t, ATTN_WIDTH - ROPE_HALF, 1) * sin_first
                + pltpu.roll(t, ROPE_HALF, 1) * sin_second)

    def emit_layouts(t, out_refs):
        for s in range(ATTN_SLABS):
            slab_ref[0, s] = t[:, s * LANES:(s + 1) * LANES]
        src, d_prev = 0, 1
        for d, out_ref in zip(DILATIONS, out_refs):
            if d == 1:
                out_ref[...] = t.astype(BF16)
                continue
            n, n_prev, ratio = tm // d, tm // d_prev, d // d_prev
            keep = d != DILATIONS[-1]
            for r in range(d):
                start = (r % d_prev) * n_prev + r // d_prev
                part = jnp.concatenate([slab_ref[src, s, pl.ds(start, n, stride=ratio), :]
                                        for s in range(ATTN_SLABS)], axis=1)
                out_ref[:, r * ATTN_WIDTH:(r + 1) * ATTN_WIDTH] = part.astype(BF16)
                if keep:
                    for s in range(ATTN_SLABS):
                        slab_ref[1 - src, s, r * n:(r + 1) * n, :] = part[:, s * LANES:(s + 1) * LANES]
            src, d_prev = 1 - src, d

    g_low = _dot(xb, wlow_ref[...])
    z = _dot(g_low.astype(BF16), wup_ref[...]) + bias_ref[...]
    log_sig = jnp.minimum(z, 0.0) - jnp.log1p(jnp.exp(-jnp.abs(z)))
    glog_ref[...] = log_sig * (LOG2_E / GLA_TAU)

    n_d = len(DILATIONS)
    c0 = 0
    emit_layouts(rope(proj(c0, ATTN_WIDTH)) * (HEAD_DIM ** -0.5 * LOG2_E), qkv_refs[0:n_d])
    c0 += ATTN_WIDTH
    emit_layouts(rope(proj(c0, ATTN_WIDTH)), qkv_refs[n_d:2 * n_d])
    c0 += ATTN_WIDTH
    emit_layouts(proj(c0, ATTN_WIDTH), qkv_refs[2 * n_d:3 * n_d])
    c0 += ATTN_WIDTH
    gq_ref[...] = (proj(c0, GLA_K_WIDTH) * (GLA_DK ** -0.5)).astype(BF16)
    c0 += GLA_K_WIDTH
    gk_ref[...] = proj(c0, GLA_K_WIDTH).astype(BF16)
    c0 += GLA_K_WIDTH
    gv_ref[...] = proj(c0, GLA_V_WIDTH).astype(BF16)
    c0 += GLA_V_WIDTH
    gr_ref[...] = proj(c0, GLA_V_WIDTH).astype(BF16)


def _inproj(h, layer, gain, w_in, w_low, w_up, bias, inv_freq, rope_offsets):
    S = h.shape[0]
    tm = rope_offsets.shape[1]
    qkv_specs, qkv_shapes = [], []
    for _ in range(3):
        for d in DILATIONS:
            qkv_specs.append(_rows(tm // d, d * ATTN_WIDTH))
            qkv_shapes.append(jax.ShapeDtypeStruct((S // d, d * ATTN_WIDTH), BF16))
    gla_widths = (GLA_K_WIDTH,) * 2 + (GLA_V_WIDTH,) * 2
    out_shape = qkv_shapes + [jax.ShapeDtypeStruct((S, w), BF16) for w in gla_widths] + [jax.ShapeDtypeStruct((S, GLA_K_WIDTH), F32)]
    resident = 2 * (D_MODEL * MAIN_WIDTH + 2 * D_MODEL * LANES)
    streamed = tm * (D_MODEL * 4 + 3 * len(DILATIONS) * ATTN_WIDTH * 2 + MAIN_WIDTH * 2) + tm * MAIN_WIDTH * 4
    return pl.pallas_call(
        functools.partial(_inproj_kernel, tm=tm),
        grid=(S // tm,),
        in_specs=[_rows(tm, D_MODEL), _layer_block(layer, (1, D_MODEL)), _layer_block(layer, (D_MODEL, MAIN_WIDTH)),
                  _layer_block(layer, (D_MODEL, LANES)), _layer_block(layer, (LANES, GLA_K_WIDTH)),
                  _layer_block(layer, (1, GLA_K_WIDTH)), _resident((1, LANES)), _resident((2, tm, LANES))],
        out_specs=qkv_specs + [_rows(tm, w) for w in gla_widths] + [_rows(tm, GLA_K_WIDTH)],
        out_shape=out_shape,
        scratch_shapes=[pltpu.VMEM((2, ATTN_SLABS, tm, LANES), F32)],
        compiler_params=pltpu.CompilerParams(dimension_semantics=("arbitrary",),
                                             vmem_limit_bytes=_vmem_limit(resident, streamed)),
        name="inproj",
    )(h, gain, w_in, w_low, w_up, bias, inv_freq, rope_offsets)


def _dilattn_kernel(q_ref, kc_ref, kp_ref, vc_ref, vp_ref, o_ref, stat_ref, *, tq):
    first_tile = pl.program_id(1) == 0
    sub = ATTN_SUB
    grp = MXU_DIM
    a = lax.broadcasted_iota(jnp.int32, (sub, 2 * sub), 0)
    c = lax.broadcasted_iota(jnp.int32, (sub, 2 * sub), 1)
    band = (c >= a) & (c <= a + WINDOW_BACK)
    band_first = band & ((c >= sub) | jnp.logical_not(first_tile))
    q_head = lax.broadcasted_iota(jnp.int32, (sub, grp), 1) // HEAD_DIM

    for j in range(tq // sub):
        rows = slice(j * sub, (j + 1) * sub)
        valid = jnp.concatenate([band_first if j == 0 else band] * HEADS_PER_GROUP, axis=0)
        for g in range(ATTN_GROUPS):
            cols = slice(g * grp, (g + 1) * grp)
            qg = q_ref[rows, cols]
            if j == 0:
                kg = jnp.concatenate([kp_ref[:, cols], kc_ref[0:sub, cols]], axis=0)
                vg = jnp.concatenate([vp_ref[:, cols], vc_ref[0:sub, cols]], axis=0)
            else:
                kg = kc_ref[(j - 1) * sub:(j + 1) * sub, cols]
                vg = vc_ref[(j - 1) * sub:(j + 1) * sub, cols]
            qs = jnp.concatenate([jnp.where(q_head == hh, qg, jnp.zeros_like(qg))
                                  for hh in range(HEADS_PER_GROUP)], axis=0)
            s = jnp.where(valid, _dot_nt(qs, kg), NEG_BIG)
            m = jnp.max(s, axis=-1, keepdims=True)
            p = jnp.exp2(s - m)
            sums = jnp.sum(p, axis=-1, keepdims=True)
            pv = _dot(p.astype(BF16), vg)
            for hh in range(HEADS_PER_GROUP):
                hr = slice(hh * sub, (hh + 1) * sub)
                head = g * HEADS_PER_GROUP + hh
                lanes = slice(hh * HEAD_DIM, (hh + 1) * HEAD_DIM)
                o_ref[rows, g * grp + hh * HEAD_DIM:g * grp + (hh + 1) * HEAD_DIM] = pv[hr, lanes]
                stat_ref[rows, head:head + 1] = m[hr]
                stat_ref[rows, STAT_SUM_LANE + head:STAT_SUM_LANE + head + 1] = sums[hr]
        stat_ref[rows, 2 * ATTN_HEADS:] = jnp.zeros((sub, LANES - 2 * ATTN_HEADS), F32)


def _dilattn(q, k, v, d):
    rows = q.shape[0]
    tq = min(ATTN_Q_TILE, rows)
    W = ATTN_WIDTH
    cur = pl.BlockSpec((tq, W), lambda r, i: (i, r))
    prev = pl.BlockSpec((ATTN_SUB, W), lambda r, i: (jnp.maximum(i * (tq // ATTN_SUB) - 1, 0), r))
    streamed = tq * W * (3 * 2 + 4) + 2 * ATTN_SUB * W * 2 + tq * LANES * 4
    return pl.pallas_call(
        functools.partial(_dilattn_kernel, tq=tq),
        grid=(d, rows // tq),
        in_specs=[cur, cur, prev, cur, prev],
        out_specs=[pl.BlockSpec((tq, W), lambda r, i: (i, r)), pl.BlockSpec((tq, LANES), lambda r, i: (i, r))],
        out_shape=[jax.ShapeDtypeStruct((rows, d * W), F32), jax.ShapeDtypeStruct((rows, d * LANES), F32)],
        compiler_params=pltpu.CompilerParams(dimension_semantics=("arbitrary", "arbitrary"),
                                             vmem_limit_bytes=_vmem_limit(0, streamed)),
        name=f"dilattn{d}",
    )(q, k, k, v, v)


def _merge_patterns(o_refs, st_refs, expand_ref, gain_ref, oslab_ref, stslab_ref, tm):
    def natural_rows(ref, d, width, slab_ref):
        if d == 1:
            return ref[...]
        n = tm // d
        slabs = width // LANES
        for r in range(d):
            for s in range(slabs):
                slab_ref[s, pl.ds(r, n, stride=d), :] = ref[:, r * width + s * LANES:r * width + (s + 1) * LANES]
        return jnp.concatenate([slab_ref[s] for s in range(slabs)], axis=1)

    stats = [natural_rows(st_refs[i], d, LANES, stslab_ref.at[i]) for i, d in enumerate(DILATIONS)]
    lane = lax.broadcasted_iota(jnp.int32, (tm, LANES), 1)
    top = functools.reduce(jnp.maximum, stats)
    scale = [jnp.exp2(st - top) for st in stats]
    total = sum(sc * pltpu.roll(st, LANES - STAT_SUM_LANE, 1) for sc, st in zip(scale, stats))
    inv_total = jnp.where(lane < ATTN_HEADS, 1.0 / total, 0.0)
    acc = None
    for i, d in enumerate(DILATIONS):
        parts = _split3(scale[i] * inv_total)[:MERGE_COEF_PIECES]
        coef = sum(_dot(p, expand_ref[...]) for p in parts)
        term = coef * natural_rows(o_refs[i], d, ATTN_WIDTH, oslab_ref)
        acc = term if acc is None else acc + term
    return _rms(acc, gain_ref[...]).astype(BF16)


def _gla_tables(C):
    levels = int(np.log2(C))
    t = np.arange(C)[:, None]
    j = np.arange(C)[None, :]
    full, compact = [], []
    for l in range(levels):
        c = 1 << l
        second = (t % (2 * c)) >= c
        mask = (t // (2 * c) == j // (2 * c)) & second & ((j % (2 * c)) < c)
        if c < GLA_ROW_SPLIT_MIN:
            full.append(mask)
        else:
            compact.append(mask[second[:, 0]])
    full.append(t == j)
    seg = (j <= t).astype(np.float32)
    tiled = lambda ms: np.stack([np.tile(mm, (GLA_HEADS, 1)) for mm in ms]).astype(np.float32)
    return seg, tiled(full), tiled(compact)


def _second_half_rows(C, c):
    return [(start, c) for start in range(c, C, 2 * c)]


def _gla_kernel(q_ref, k_ref, v_ref, g_ref, r_ref, gain_ref, seg_ref, msk_ref, mskq_ref, *refs, tb):
    n_cast = (len(refs) - 2) // 2
    cast_in_refs, o_ref, cast_out_refs, state_ref = refs[:n_cast], refs[n_cast], refs[n_cast + 1:-1], refs[-1]
    for src_ref, dst_ref in zip(cast_in_refs, cast_out_refs):
        dst_ref[...] = src_ref[...].astype(BF16)
    C = GLA_CHUNK
    levels = msk_ref.shape[0] - 1 + mskq_ref.shape[0]
    row = lax.broadcasted_iota(jnp.int32, (C, GLA_K_WIDTH), 0)

    @pl.when(pl.program_id(0) == 0)
    def _():
        state_ref[...] = jnp.zeros_like(state_ref)

    def stack_heads(t):
        keep = lax.broadcasted_iota(jnp.int32, t.shape, 1) // GLA_DK
        return jnp.concatenate([jnp.where(keep == hh, t, jnp.zeros_like(t)) for hh in range(GLA_HEADS)], axis=0)

    def add_rows(scores, upd, ranges):
        n = sum(size for _, size in ranges)
        pieces = []
        for hh in range(GLA_HEADS):
            pos, off = 0, 0
            for start, size in ranges:
                if start > pos:
                    pieces.append(scores[hh * C + pos:hh * C + start])
                pieces.append(scores[hh * C + start:hh * C + start + size] + upd[hh * n + off:hh * n + off + size])
                pos, off = start + size, off + size
            if pos < C:
                pieces.append(scores[hh * C + pos:(hh + 1) * C])
        return jnp.concatenate(pieces, axis=0)

    for ci in range(tb // C):
        rows = slice(ci * C, (ci + 1) * C)
        q = q_ref[rows, :].astype(F32)
        k = k_ref[rows, :].astype(F32)
        v = v_ref[rows, :]
        b = sum(_dot(seg_ref[...], part) for part in _split3(g_ref[rows, :])[:GLA_DECAY_PIECES])
        u_from_start = jnp.exp2(b)
        u_to_end = jnp.exp2(b[C - 1:C, :] - b)

        n_small = msk_ref.shape[0] - 1
        scores = msk_ref[n_small] * _dot_nt(stack_heads(q.astype(BF16)), k.astype(BF16))
        block_end = b
        for l in range(levels):
            c = 1 << l
            first_half = (row & c) == 0
            b_mid = jnp.where(first_half, block_end, pltpu.roll(block_end, c, 0))
            ul = jnp.exp2(-jnp.abs(b - b_mid))
            keys = (k * ul).astype(BF16)
            if l < n_small:
                scores = scores + msk_ref[l] * _dot_nt(stack_heads((q * ul).astype(BF16)), keys)
            else:
                ranges = _second_half_rows(C, c)
                queries = jnp.concatenate([q[st:st + n] * ul[st:st + n] for st, n in ranges], axis=0)
                upd = mskq_ref[l - n_small] * _dot_nt(stack_heads(queries.astype(BF16)), keys)
                scores = add_rows(scores, upd, ranges)
            if l + 1 < levels:
                block_end = jnp.where(first_half, pltpu.roll(block_end, C - c, 0), block_end)

        state_t = state_ref[...]
        inter = _dot_nt(stack_heads((q * u_from_start).astype(BF16)), state_t.astype(BF16))
        scores = scores.astype(BF16)
        gain = gain_ref[...]
        for hh in range(GLA_HEADS):
            hr = slice(hh * C, (hh + 1) * C)
            vc = slice(hh * GLA_DV, (hh + 1) * GLA_DV)
            o_h = _dot(scores[hr], v[:, vc]) + inter[hr]
            o_h = _rms(o_h, gain[:, vc])
            gate = r_ref[rows, vc].astype(F32)
            o_ref[rows, vc] = (o_h * (gate * jax.nn.sigmoid(gate))).astype(BF16)

        upd = _dot_tn(v, (k * u_to_end).astype(BF16))
        decayed = state_t * u_from_start[C - 1:C, :]
        for hh in range(GLA_HEADS):
            lanes = slice(hh * GLA_DK, (hh + 1) * GLA_DK)
            state_ref[:, lanes] = decayed[:, lanes] + upd[hh * GLA_DV:(hh + 1) * GLA_DV, lanes]


def _gla(gq, gk, gv, glog, gr, layer, gain, later_weights):
    S = gq.shape[0]
    tb = GLA_BLOCK
    steps = S // tb
    seg, msk, mskq = _gla_tables(GLA_CHUNK)
    cast_in_specs, cast_out_specs, cast_shapes, cast_bytes = [], [], [], 0
    for w in later_weights:
        _, rows, cols = w.shape
        assert rows % (steps * 16) == 0
        share = rows // steps
        cast_in_specs.append(pl.BlockSpec((None, share, cols), lambda i: (layer, i, 0)))
        cast_out_specs.append(_rows(share, cols))
        cast_shapes.append(jax.ShapeDtypeStruct((rows, cols), BF16))
        cast_bytes += share * cols * (4 + 2)
    resident = seg.size * 2 + (msk.size + mskq.size) * 4
    streamed = tb * (2 * GLA_K_WIDTH * 2 + 2 * GLA_V_WIDTH * 2 + GLA_K_WIDTH * 4 + GLA_V_WIDTH * 2) + cast_bytes
    return pl.pallas_call(
        functools.partial(_gla_kernel, tb=tb),
        grid=(steps,),
        in_specs=[_rows(tb, GLA_K_WIDTH), _rows(tb, GLA_K_WIDTH), _rows(tb, GLA_V_WIDTH), _rows(tb, GLA_K_WIDTH),
                  _rows(tb, GLA_V_WIDTH), _layer_block(layer, (1, GLA_V_WIDTH)), _resident(seg.shape),
                  _resident(msk.shape), _resident(mskq.shape)] + cast_in_specs,
        out_specs=[_rows(tb, GLA_V_WIDTH)] + cast_out_specs,
        out_shape=[jax.ShapeDtypeStruct((S, GLA_V_WIDTH), BF16)] + cast_shapes,
        scratch_shapes=[pltpu.VMEM((GLA_DV, GLA_K_WIDTH), F32)],
        compiler_params=pltpu.CompilerParams(dimension_semantics=("arbitrary",),
                                             vmem_limit_bytes=_vmem_limit(resident, streamed)),
        name="gla",
    )(gq, gk, gv, glog, gr, gain, jnp.asarray(seg, BF16), jnp.asarray(msk, F32), jnp.asarray(mskq, F32),
      *later_weights)


def _mixffn_kernel(*refs, tm):
    n_d = len(DILATIONS)
    o_refs, st_refs = refs[:n_d], refs[n_d:2 * n_d]
    (g_ref, h_ref, expand_ref, again_ref, wa_ref, wg_ref, mpost_ref, fpre_ref, wgate_ref, wup_ref, wdown_ref,
     fpost_ref, out_ref, oslab_ref, stslab_ref) = refs[2 * n_d:]
    a_out = _merge_patterns(o_refs, st_refs, expand_ref, again_ref, oslab_ref, stslab_ref, tm)
    m = _dot(a_out, wa_ref[...]) + _dot(g_ref[...], wg_ref[...])
    h = h_ref[...] + _rms(m, mpost_ref[...])
    x = _rms(h, fpre_ref[...]).astype(BF16)
    acc = None
    for c0 in range(0, D_FF, FF_CHUNK):
        cols = slice(c0, c0 + FF_CHUNK)
        gate = _dot(x, wgate_ref[:, cols])
        act = (gate * jax.nn.sigmoid(gate) * _dot(x, wup_ref[:, cols])).astype(BF16)
        part = _dot(act, wdown_ref[cols, :])
        acc = part if acc is None else acc + part
    out_ref[...] = h + _rms(acc, fpost_ref[...])


def _mixffn(os_, stats, g_out, h, layer, attn_gain, w_out, mix_post, ffn_pre, w_gate, w_up, w_down, ffn_post):
    S = h.shape[0]
    tm = ROW_TILE
    expand = np.zeros((LANES, ATTN_WIDTH), np.float32)
    for hd in range(ATTN_HEADS):
        expand[hd, hd * HEAD_DIM:(hd + 1) * HEAD_DIM] = 1.0
    resident = 2 * (MIX_WIDTH * D_MODEL + 3 * D_MODEL * D_FF) + (ATTN_SLABS + len(DILATIONS)) * tm * LANES * 4
    streamed = tm * (len(DILATIONS) * (ATTN_WIDTH + LANES) * 4 + GLA_V_WIDTH * 2 + 2 * D_MODEL * 4)
    return pl.pallas_call(
        functools.partial(_mixffn_kernel, tm=tm),
        grid=(S // tm,),
        in_specs=([_rows(tm // d, d * ATTN_WIDTH) for d in DILATIONS] + [_rows(tm // d, d * LANES) for d in DILATIONS]
                  + [_rows(tm, GLA_V_WIDTH), _rows(tm, D_MODEL), _resident((LANES, ATTN_WIDTH)),
                     _layer_block(layer, (1, ATTN_WIDTH)),
                     pl.BlockSpec((ATTN_WIDTH, D_MODEL), lambda i: (0, 0), pipeline_mode=pl.Buffered(1)),
                     pl.BlockSpec((GLA_V_WIDTH, D_MODEL), lambda i: (ATTN_WIDTH // GLA_V_WIDTH, 0),
                                  pipeline_mode=pl.Buffered(1)),
                     _layer_block(layer, (1, D_MODEL)), _layer_block(layer, (1, D_MODEL)),
                     _resident((D_MODEL, D_FF)), _resident((D_MODEL, D_FF)),
                     _resident((D_FF, D_MODEL)), _layer_block(layer, (1, D_MODEL))]),
        out_specs=_rows(tm, D_MODEL),
        out_shape=jax.ShapeDtypeStruct((S, D_MODEL), F32),
        scratch_shapes=[pltpu.VMEM((ATTN_SLABS, tm, LANES), F32), pltpu.VMEM((len(DILATIONS), 1, tm, LANES), F32)],
        compiler_params=pltpu.CompilerParams(dimension_semantics=("arbitrary",),
                                             vmem_limit_bytes=_vmem_limit(resident, streamed)),
        name="mixffn",
    )(*os_, *stats, g_out, h, jnp.asarray(expand, BF16), attn_gain, w_out, w_out, mix_post, ffn_pre,
      w_gate, w_up, w_down, ffn_post)


def _rope_inv_freq():
    inv = np.zeros((1, LANES), np.float32)
    freqs = np.asarray(ROPE_THETA, np.float32) ** (-(np.arange(ROPE_HALF, dtype=np.float32) * 2.0) / ROPE_DIM)
    for lane in range(LANES):
        if lane % HEAD_DIM < ROPE_DIM:
            inv[0, lane] = freqs[lane % ROPE_HALF]
    return jnp.asarray(inv)


def kernel(x, mix_pre_norm, mix_post_norm, ffn_pre_norm, ffn_post_norm, w_in, gla_w_gate_up, gla_b_gate,
           gla_out_norm, attn_out_norm, w_out, w_gate, w_up, w_down):
    B, S, D = x.shape
    depth = mix_pre_norm.shape[0]
    assert D == D_MODEL and S % (max(DILATIONS) * ATTN_SUB) == 0
    assert S % GLA_BLOCK == 0 and S % ROW_TILE == 0 and S % INPROJ_TILE == 0
    assert DILATIONS[0] == 1 and all(b % a == 0 for a, b in zip(DILATIONS, DILATIONS[1:]))
    inv_freq = _rope_inv_freq()
    rope_offsets = _rope_offsets(inv_freq, INPROJ_TILE)
    gains = lambda t: t.reshape(depth, 1, -1)
    mix_pre, mix_post, ffn_pre, ffn_post = gains(mix_pre_norm), gains(mix_post_norm), gains(ffn_pre_norm), gains(ffn_post_norm)
    gla_norm, attn_norm, b_gate = gains(gla_out_norm), gains(attn_out_norm), gains(gla_b_gate)
    w_in_b = w_in.astype(BF16)
    w_low = jnp.pad(w_in[:, :, MAIN_WIDTH:], ((0, 0), (0, 0), (0, LANES - GLA_GATE_RANK))).astype(BF16)
    w_gu = jnp.pad(gla_w_gate_up, ((0, 0), (0, LANES - GLA_GATE_RANK), (0, 0))).astype(BF16)

    outs = []
    for b in range(B):
        h = x[b]
        for l in range(depth):
            res = _inproj(h, l, mix_pre, w_in_b, w_low, w_gu, b_gate, inv_freq, rope_offsets)
            n_d = len(DILATIONS)
            qs, ks, vs = res[0:n_d], res[n_d:2 * n_d], res[2 * n_d:3 * n_d]
            gq, gk, gv, gr, glog = res[3 * n_d:]
            pats = [_dilattn(qs[i], ks[i], vs[i], d) for i, d in enumerate(DILATIONS)]
            g_out, w_out_b, w_gate_b, w_up_b, w_down_b = _gla(gq, gk, gv, glog, gr, l, gla_norm,
                                                               (w_out, w_gate, w_up, w_down))
            h = _mixffn([p[0] for p in pats], [p[1] for p in pats], g_out, h, l, attn_norm, w_out_b, mix_post,
                        ffn_pre, w_gate_b, w_up_b, w_down_b, ffn_post)
        outs.append(h)
    return jnp.stack(outs, axis=0)
```

```python
import functools

import numpy as np
import jax
import jax.numpy as jnp
from jax import lax
from jax.experimental import pallas as pl
from jax.experimental.pallas import tpu as pltpu

F32 = jnp.float32
BF16 = jnp.bfloat16

D_MODEL = 1024
HEAD_DIM = 64
ATTN_HEADS = 8
ATTN_WIDTH = ATTN_HEADS * HEAD_DIM
GLA_HEADS = 4
GLA_DK = 64
GLA_DV = 128
GLA_K_WIDTH = GLA_HEADS * GLA_DK
GLA_V_WIDTH = GLA_HEADS * GLA_DV
GLA_GATE_RANK = 16
GLA_TAU = 16.0
MIX_WIDTH = ATTN_WIDTH + GLA_V_WIDTH
D_FF = 2816
ROPE_THETA = 500000.0
ROPE_DIM = HEAD_DIM // 4
ROPE_HALF = ROPE_DIM // 2
DILATIONS = (1, 4, 16)
WINDOW_BACK = 128
RMS_EPS = 1e-6
MAIN_WIDTH = 3 * ATTN_WIDTH + 2 * GLA_K_WIDTH + 2 * GLA_V_WIDTH

LANES = 128
MXU_DIM = 256
VMEM_BYTES_V7X = 64 * 1024 * 1024
VMEM_REQUEST_FLOOR = 52 * 1024 * 1024

ROW_TILE = 512
INPROJ_TILE = 1024
ATTN_Q_TILE = 2048
ATTN_SUB = WINDOW_BACK
HEADS_PER_GROUP = MXU_DIM // HEAD_DIM
ATTN_GROUPS = ATTN_WIDTH // MXU_DIM
ATTN_SLABS = ATTN_WIDTH // LANES
STAT_SUM_LANE = ATTN_HEADS
GLA_CHUNK = 128
GLA_BLOCK = 1024
GLA_ROW_SPLIT_MIN = 8
FF_CHUNK = 256
NEG_BIG = -1e30
MERGE_MAX_STRIDE = 4
MERGE_COEF_PIECES = 1
GLA_DECAY_PIECES = 2
LOG2_E = 1.4426950408889634


def _dot(a, b):
    return jnp.dot(a, b, preferred_element_type=F32)


def _dot_nt(a, b):
    return lax.dot_general(a, b, (((1,), (1,)), ((), ())), preferred_element_type=F32)


def _dot_tn(a, b):
    return lax.dot_general(a, b, (((0,), (0,)), ((), ())), preferred_element_type=F32)


def _rms(x, gain):
    return x * lax.rsqrt(jnp.mean(x * x, axis=-1, keepdims=True) + RMS_EPS) * gain


def _split3(x):
    hi = x.astype(BF16)
    r1 = x - hi.astype(F32)
    mid = r1.astype(BF16)
    lo = (r1 - mid.astype(F32)).astype(BF16)
    return hi, mid, lo


def _vmem_limit(resident_bytes, streamed_bytes):
    need = 2 * (resident_bytes + 2 * streamed_bytes)
    return int(min(max(need, VMEM_REQUEST_FLOOR), VMEM_BYTES_V7X - 8 * 1024 * 1024))


def _resident(shape):
    nd = len(shape)
    return pl.BlockSpec(shape, lambda *_: (0,) * nd, pipeline_mode=pl.Buffered(1))


def _layer_block(layer, shape, index=None):
    index = (0,) * len(shape) if index is None else index
    return pl.BlockSpec((None,) + tuple(shape), lambda *_: (layer,) + tuple(index), pipeline_mode=pl.Buffered(1))


def _rows(tm, width):
    return pl.BlockSpec((tm, width), lambda i: (i, 0))


def _rope_offsets_kernel(inv_ref, rot_ref):
    off = lax.broadcasted_iota(jnp.int32, rot_ref.shape[1:], 0).astype(F32) * inv_ref[...]
    rot_ref[0] = jnp.cos(off)
    rot_ref[1] = jnp.sin(off)


def _rope_offsets(inv_freq, tm):
    return pl.pallas_call(_rope_offsets_kernel, out_shape=jax.ShapeDtypeStruct((2, tm, LANES), F32),
                          name="ropeoffsets")(inv_freq)


def _inproj_kernel(x_ref, gain_ref, w_ref, wlow_ref, wup_ref, bias_ref, inv_ref, rot_ref, *refs, tm):
    qkv_refs = refs[:3 * len(DILATIONS)]
    gq_ref, gk_ref, gv_ref, gr_ref, glog_ref, slab_ref = refs[3 * len(DILATIONS):]
    xb = _rms(x_ref[...], gain_ref[...]).astype(BF16)

    def proj(c0, width):
        return _dot_nt(xb, w_ref[c0:c0 + width, :])

    base = (pl.program_id(0) * tm).astype(F32) * inv_ref[...]
    cos_b, sin_b = jnp.cos(base), jnp.sin(base)
    cos = cos_b * rot_ref[0] - sin_b * rot_ref[1]
    sin = sin_b * rot_ref[0] + cos_b * rot_ref[1]
    lane = lax.broadcasted_iota(jnp.int32, (tm, LANES), 1) % HEAD_DIM
    cosf = jnp.concatenate([cos] * ATTN_SLABS, axis=1)
    sin_first = jnp.concatenate([jnp.where(lane < ROPE_HALF, -sin, 0.0)] * ATTN_SLABS, axis=1)
    sin_second = jnp.concatenate([jnp.where(lane >= ROPE_HALF, sin, 0.0)] * ATTN_SLABS, axis=1)

    def rope(t):
        return (t * cosf + pltpu.roll(t, ATTN_WIDTH - ROPE_HALF, 1) * sin_first
                + pltpu.roll(t, ROPE_HALF, 1) * sin_second)

    def emit_layouts(t, out_refs):
        for s in range(ATTN_SLABS):
            slab_ref[0, s] = t[:, s * LANES:(s + 1) * LANES]
        src, d_prev = 0, 1
        for d, out_ref in zip(DILATIONS, out_refs):
            if d == 1:
                out_ref[...] = t.astype(BF16)
                continue
            n, n_prev, ratio = tm // d, tm // d_prev, d // d_prev
            keep = d != DILATIONS[-1]
            for r in range(d):
                start = (r % d_prev) * n_prev + r // d_prev
                part = jnp.concatenate([slab_ref[src, s, pl.ds(start, n, stride=ratio), :]
                                        for s in range(ATTN_SLABS)], axis=1)
                out_ref[:, r * ATTN_WIDTH:(r + 1) * ATTN_WIDTH] = part.astype(BF16)
                if keep:
                    for s in range(ATTN_SLABS):
                        slab_ref[1 - src, s, r * n:(r + 1) * n, :] = part[:, s * LANES:(s + 1) * LANES]
            src, d_prev = 1 - src, d

    g_low = _dot_nt(xb, wlow_ref[...])
    z = _dot(g_low.astype(BF16), wup_ref[...]) + bias_ref[...]
    log_sig = jnp.minimum(z, 0.0) - jnp.log1p(jnp.exp(-jnp.abs(z)))
    glog_ref[...] = log_sig * (LOG2_E / GLA_TAU)

    n_d = len(DILATIONS)
    c0 = 0
    emit_layouts(rope(proj(c0, ATTN_WIDTH)) * (HEAD_DIM ** -0.5 * LOG2_E), qkv_refs[0:n_d])
    c0 += ATTN_WIDTH
    emit_layouts(rope(proj(c0, ATTN_WIDTH)), qkv_refs[n_d:2 * n_d])
    c0 += ATTN_WIDTH
    emit_layouts(proj(c0, ATTN_WIDTH), qkv_refs[2 * n_d:3 * n_d])
    c0 += ATTN_WIDTH
    gq_ref[...] = (proj(c0, GLA_K_WIDTH) * (GLA_DK ** -0.5)).astype(BF16)
    c0 += GLA_K_WIDTH
    gk_ref[...] = proj(c0, GLA_K_WIDTH).astype(BF16)
    c0 += GLA_K_WIDTH
    gv_ref[...] = proj(c0, GLA_V_WIDTH).astype(BF16)
    c0 += GLA_V_WIDTH
    gr_ref[...] = proj(c0, GLA_V_WIDTH).astype(BF16)


def _inproj(h, layer, gain, w_in, w_low, w_up, bias, inv_freq, rope_offsets):
    S = h.shape[0]
    tm = rope_offsets.shape[1]
    qkv_specs, qkv_shapes = [], []
    for _ in range(3):
        for d in DILATIONS:
            qkv_specs.append(_rows(tm // d, d * ATTN_WIDTH))
            qkv_shapes.append(jax.ShapeDtypeStruct((S // d, d * ATTN_WIDTH), BF16))
    gla_widths = (GLA_K_WIDTH,) * 2 + (GLA_V_WIDTH,) * 2
    out_shape = qkv_shapes + [jax.ShapeDtypeStruct((S, w), BF16) for w in gla_widths] + [jax.ShapeDtypeStruct((S, GLA_K_WIDTH), F32)]
    resident = 2 * (D_MODEL * MAIN_WIDTH + 2 * D_MODEL * LANES)
    streamed = tm * (D_MODEL * 4 + 3 * len(DILATIONS) * ATTN_WIDTH * 2 + MAIN_WIDTH * 2) + tm * MAIN_WIDTH * 4
    return pl.pallas_call(
        functools.partial(_inproj_kernel, tm=tm),
        grid=(S // tm,),
        in_specs=[_rows(tm, D_MODEL), _layer_block(layer, (1, D_MODEL)), _layer_block(layer, (MAIN_WIDTH, D_MODEL)),
                  _layer_block(layer, (LANES, D_MODEL)), _layer_block(layer, (LANES, GLA_K_WIDTH)),
                  _layer_block(layer, (1, GLA_K_WIDTH)), _resident((1, LANES)), _resident((2, tm, LANES))],
        out_specs=qkv_specs + [_rows(tm, w) for w in gla_widths] + [_rows(tm, GLA_K_WIDTH)],
        out_shape=out_shape,
        scratch_shapes=[pltpu.VMEM((2, ATTN_SLABS, tm, LANES), F32)],
        compiler_params=pltpu.CompilerParams(dimension_semantics=("arbitrary",),
                                             vmem_limit_bytes=_vmem_limit(resident, streamed)),
        name="inproj",
    )(h, gain, w_in, w_low, w_up, bias, inv_freq, rope_offsets)


def _dilattn_kernel(q_ref, kc_ref, kp_ref, vc_ref, vp_ref, o_ref, stat_ref, *, tq, seqs):
    first_tile = pl.program_id(1) == 0
    sub = ATTN_SUB
    grp = MXU_DIM
    a = lax.broadcasted_iota(jnp.int32, (sub, 2 * sub), 0)
    c = lax.broadcasted_iota(jnp.int32, (sub, 2 * sub), 1)
    band = (c >= a) & (c <= a + WINDOW_BACK)
    band_first = band & ((c >= sub) | jnp.logical_not(first_tile))
    q_head = lax.broadcasted_iota(jnp.int32, (sub, grp), 1) // HEAD_DIM

    for seq in range(seqs):
        for j in range(tq // sub):
            rows = slice(j * sub, (j + 1) * sub)
            valid = jnp.concatenate([band_first if j == 0 else band] * HEADS_PER_GROUP, axis=0)
            for g in range(ATTN_GROUPS):
                cols = slice(seq * ATTN_WIDTH + g * grp, seq * ATTN_WIDTH + (g + 1) * grp)
                qg = q_ref[rows, cols]
                if j == 0:
                    kg = jnp.concatenate([kp_ref[:, cols], kc_ref[0:sub, cols]], axis=0)
                    vg = jnp.concatenate([vp_ref[:, cols], vc_ref[0:sub, cols]], axis=0)
                else:
                    kg = kc_ref[(j - 1) * sub:(j + 1) * sub, cols]
                    vg = vc_ref[(j - 1) * sub:(j + 1) * sub, cols]
                qs = jnp.concatenate([jnp.where(q_head == hh, qg, jnp.zeros_like(qg))
                                      for hh in range(HEADS_PER_GROUP)], axis=0)
                s = jnp.where(valid, _dot_nt(qs, kg), NEG_BIG)
                m = jnp.max(s, axis=-1, keepdims=True)
                p = jnp.exp2(s - m)
                sums = jnp.sum(p, axis=-1, keepdims=True)
                pv = _dot(p.astype(BF16), vg)
                for hh in range(HEADS_PER_GROUP):
                    hr = slice(hh * sub, (hh + 1) * sub)
                    head = seq * LANES + g * HEADS_PER_GROUP + hh
                    out0 = seq * ATTN_WIDTH + g * grp + hh * HEAD_DIM
                    o_ref[rows, out0:out0 + HEAD_DIM] = pv[hr, hh * HEAD_DIM:(hh + 1) * HEAD_DIM]
                    stat_ref[rows, head:head + 1] = m[hr]
                    stat_ref[rows, STAT_SUM_LANE + head:STAT_SUM_LANE + head + 1] = sums[hr]
            stat_ref[rows, seq * LANES + 2 * ATTN_HEADS:(seq + 1) * LANES] = jnp.zeros((sub, LANES - 2 * ATTN_HEADS), F32)


def _dilattn(q, k, v, d):
    rows = q.shape[0]
    tq = min(ATTN_Q_TILE, rows)
    seqs = min(d, ATTN_Q_TILE // tq)
    W = seqs * ATTN_WIDTH
    cur = pl.BlockSpec((tq, W), lambda r, i: (i, r))
    prev = pl.BlockSpec((ATTN_SUB, W), lambda r, i: (jnp.maximum(i * (tq // ATTN_SUB) - 1, 0), r))
    streamed = tq * W * (3 * 2 + 4) + 2 * ATTN_SUB * W * 2 + tq * seqs * LANES * 4
    return pl.pallas_call(
        functools.partial(_dilattn_kernel, tq=tq, seqs=seqs),
        grid=(d // seqs, rows // tq),
        in_specs=[cur, cur, prev, cur, prev],
        out_specs=[pl.BlockSpec((tq, W), lambda r, i: (i, r)), pl.BlockSpec((tq, seqs * LANES), lambda r, i: (i, r))],
        out_shape=[jax.ShapeDtypeStruct((rows, d * ATTN_WIDTH), F32), jax.ShapeDtypeStruct((rows, d * LANES), F32)],
        compiler_params=pltpu.CompilerParams(dimension_semantics=("arbitrary", "arbitrary"),
                                             vmem_limit_bytes=_vmem_limit(0, streamed)),
        name=f"dilattn{d}",
    )(q, k, k, v, v)


def _merge_patterns(o_refs, st_refs, expand_ref, gain_ref, oslab_ref, stslab_ref, tm):
    def natural_rows(ref, d, width, slab_ref):
        if d == 1:
            return ref[...]
        slabs = width // LANES
        strides = []
        while d > 1:
            step = min(d, MERGE_MAX_STRIDE)
            strides.append(step)
            d //= step
        blocks = [[ref[:, r * width + s * LANES:r * width + (s + 1) * LANES] for s in range(slabs)]
                  for r in range(int(np.prod(strides)))]
        n = tm // len(blocks)
        for p, step in enumerate(strides):
            dst = p % 2
            groups = len(blocks) // step
            for gq in range(groups):
                for j in range(step):
                    for s in range(slabs):
                        slab_ref[dst, s, pl.ds(gq * n * step + j, n, stride=step), :] = blocks[gq + j * groups][s]
            n *= step
            blocks = [[slab_ref[dst, s, gq * n:(gq + 1) * n, :] for s in range(slabs)] for gq in range(groups)]
        return jnp.concatenate(blocks[0], axis=1)

    stats = [natural_rows(st_refs[i], d, LANES, stslab_ref.at[i]) for i, d in enumerate(DILATIONS)]
    lane = lax.broadcasted_iota(jnp.int32, (tm, LANES), 1)
    top = functools.reduce(jnp.maximum, stats)
    scale = [jnp.exp2(st - top) for st in stats]
    total = sum(sc * pltpu.roll(st, LANES - STAT_SUM_LANE, 1) for sc, st in zip(scale, stats))
    inv_total = jnp.where(lane < ATTN_HEADS, 1.0 / total, 0.0)
    acc = None
    for i, d in enumerate(DILATIONS):
        parts = _split3(scale[i] * inv_total)[:MERGE_COEF_PIECES]
        coef = sum(_dot(p, expand_ref[...]) for p in parts)
        term = coef * natural_rows(o_refs[i], d, ATTN_WIDTH, oslab_ref)
        acc = term if acc is None else acc + term
    return _rms(acc, gain_ref[...]).astype(BF16)


def _gla_tables(C):
    levels = int(np.log2(C))
    t = np.arange(C)[:, None]
    j = np.arange(C)[None, :]
    full, compact = [], []
    for l in range(levels):
        c = 1 << l
        second = (t % (2 * c)) >= c
        mask = (t // (2 * c) == j // (2 * c)) & second & ((j % (2 * c)) < c)
        if c < GLA_ROW_SPLIT_MIN:
            full.append(mask)
        else:
            compact.append(mask[second[:, 0]])
    full.append(t == j)
    seg = (j <= t).astype(np.float32)
    tiled = lambda ms: np.stack([np.tile(mm, (GLA_HEADS, 1)) for mm in ms]).astype(np.float32)
    return seg, tiled(full), tiled(compact)


def _second_half_rows(C, c):
    return [(start, c) for start in range(c, C, 2 * c)]


def _gla_kernel(q_ref, k_ref, v_ref, g_ref, r_ref, gain_ref, seg_ref, msk_ref, mskq_ref, *refs, tb):
    n_cast = (len(refs) - 2) // 2
    cast_in_refs, o_ref, cast_out_refs, state_ref = refs[:n_cast], refs[n_cast], refs[n_cast + 1:-1], refs[-1]
    for src_ref, dst_ref in zip(cast_in_refs, cast_out_refs):
        dst_ref[...] = src_ref[...].astype(BF16)
    C = GLA_CHUNK
    levels = msk_ref.shape[0] - 1 + mskq_ref.shape[0]
    row = lax.broadcasted_iota(jnp.int32, (C, GLA_K_WIDTH), 0)

    @pl.when(pl.program_id(0) == 0)
    def _():
        state_ref[...] = jnp.zeros_like(state_ref)

    def stack_heads(t):
        keep = lax.broadcasted_iota(jnp.int32, t.shape, 1) // GLA_DK
        return jnp.concatenate([jnp.where(keep == hh, t, jnp.zeros_like(t)) for hh in range(GLA_HEADS)], axis=0)

    def add_rows(scores, upd, ranges):
        n = sum(size for _, size in ranges)
        pieces = []
        for hh in range(GLA_HEADS):
            pos, off = 0, 0
            for start, size in ranges:
                if start > pos:
                    pieces.append(scores[hh * C + pos:hh * C + start])
                pieces.append(scores[hh * C + start:hh * C + start + size] + upd[hh * n + off:hh * n + off + size])
                pos, off = start + size, off + size
            if pos < C:
                pieces.append(scores[hh * C + pos:(hh + 1) * C])
        return jnp.concatenate(pieces, axis=0)

    for ci in range(tb // C):
        rows = slice(ci * C, (ci + 1) * C)
        q = q_ref[rows, :].astype(F32)
        k = k_ref[rows, :].astype(F32)
        v = v_ref[rows, :]
        b = sum(_dot(seg_ref[...], part) for part in _split3(g_ref[rows, :])[:GLA_DECAY_PIECES])
        u_from_start = jnp.exp2(b)
        u_to_end = jnp.exp2(b[C - 1:C, :] - b)

        n_small = msk_ref.shape[0] - 1
        scores = msk_ref[n_small] * _dot_nt(stack_heads(q.astype(BF16)), k.astype(BF16))
        block_end = b
        for l in range(levels):
            c = 1 << l
            first_half = (row & c) == 0
            ul = jnp.exp2(jnp.where(first_half, block_end - b, b - pltpu.roll(block_end, c, 0)))
            keys = (k * ul).astype(BF16)
            if l < n_small:
                scores = scores + msk_ref[l] * _dot_nt(stack_heads((q * ul).astype(BF16)), keys)
            else:
                ranges = _second_half_rows(C, c)
                queries = jnp.concatenate([q[st:st + n] * ul[st:st + n] for st, n in ranges], axis=0)
                upd = mskq_ref[l - n_small] * _dot_nt(stack_heads(queries.astype(BF16)), keys)
                scores = add_rows(scores, upd, ranges)
            if l + 1 < levels:
                block_end = jnp.where(first_half, pltpu.roll(block_end, C - c, 0), block_end)

        state_t = state_ref[...]
        inter = _dot_nt(stack_heads((q * u_from_start).astype(BF16)), state_t.astype(BF16))
        scores = scores.astype(BF16)
        gain = gain_ref[...]
        for hh in range(GLA_HEADS):
            hr = slice(hh * C, (hh + 1) * C)
            vc = slice(hh * GLA_DV, (hh + 1) * GLA_DV)
            o_h = _dot(scores[hr], v[:, vc]) + inter[hr]
            o_h = _rms(o_h, gain[:, vc])
            gate = r_ref[rows, vc].astype(F32)
            o_ref[rows, vc] = (o_h * (gate * jax.nn.sigmoid(gate))).astype(BF16)

        upd = _dot_tn(v, (k * u_to_end).astype(BF16))
        decayed = state_t * u_from_start[C - 1:C, :]
        for hh in range(GLA_HEADS):
            lanes = slice(hh * GLA_DK, (hh + 1) * GLA_DK)
            state_ref[:, lanes] = decayed[:, lanes] + upd[hh * GLA_DV:(hh + 1) * GLA_DV, lanes]


def _gla(gq, gk, gv, glog, gr, layer, gain, later_weights):
    S = gq.shape[0]
    tb = GLA_BLOCK
    steps = S // tb
    seg, msk, mskq = _gla_tables(GLA_CHUNK)
    cast_in_specs, cast_out_specs, cast_shapes, cast_bytes = [], [], [], 0
    for w in later_weights:
        _, rows, cols = w.shape
        assert rows % (steps * 16) == 0
        share = rows // steps
        cast_in_specs.append(pl.BlockSpec((None, share, cols), lambda i: (layer, i, 0)))
        cast_out_specs.append(_rows(share, cols))
        cast_shapes.append(jax.ShapeDtypeStruct((rows, cols), BF16))
        cast_bytes += share * cols * (4 + 2)
    resident = seg.size * 2 + (msk.size + mskq.size) * 4
    streamed = tb * (2 * GLA_K_WIDTH * 2 + 2 * GLA_V_WIDTH * 2 + GLA_K_WIDTH * 4 + GLA_V_WIDTH * 2) + cast_bytes
    return pl.pallas_call(
        functools.partial(_gla_kernel, tb=tb),
        grid=(steps,),
        in_specs=[_rows(tb, GLA_K_WIDTH), _rows(tb, GLA_K_WIDTH), _rows(tb, GLA_V_WIDTH), _rows(tb, GLA_K_WIDTH),
                  _rows(tb, GLA_V_WIDTH), _layer_block(layer, (1, GLA_V_WIDTH)), _resident(seg.shape),
                  _resident(msk.shape), _resident(mskq.shape)] + cast_in_specs,
        out_specs=[_rows(tb, GLA_V_WIDTH)] + cast_out_specs,
        out_shape=[jax.ShapeDtypeStruct((S, GLA_V_WIDTH), BF16)] + cast_shapes,
        scratch_shapes=[pltpu.VMEM((GLA_DV, GLA_K_WIDTH), F32)],
        compiler_params=pltpu.CompilerParams(dimension_semantics=("arbitrary",),
                                             vmem_limit_bytes=_vmem_limit(resident, streamed)),
        name="gla",
    )(gq, gk, gv, glog, gr, gain, jnp.asarray(seg, BF16), jnp.asarray(msk, F32), jnp.asarray(mskq, F32),
      *later_weights)


def _mixffn_kernel(*refs, tm):
    n_d = len(DILATIONS)
    o_refs, st_refs = refs[:n_d], refs[n_d:2 * n_d]
    (g_ref, h_ref, expand_ref, again_ref, wa_ref, wg_ref, mpost_ref, fpre_ref, wgate_ref, wup_ref, wdown_ref,
     fpost_ref, out_ref, oslab_ref, stslab_ref) = refs[2 * n_d:]
    a_out = _merge_patterns(o_refs, st_refs, expand_ref, again_ref, oslab_ref, stslab_ref, tm)
    m = _dot(a_out, wa_ref[...]) + _dot(g_ref[...], wg_ref[...])
    h = h_ref[...] + _rms(m, mpost_ref[...])
    x = _rms(h, fpre_ref[...]).astype(BF16)
    acc = None
    for c0 in range(0, D_FF, FF_CHUNK):
        cols = slice(c0, c0 + FF_CHUNK)
        gate = _dot(x, wgate_ref[:, cols])
        act = (gate * jax.nn.sigmoid(gate) * _dot(x, wup_ref[:, cols])).astype(BF16)
        part = _dot(act, wdown_ref[cols, :])
        acc = part if acc is None else acc + part
    out_ref[...] = h + _rms(acc, fpost_ref[...])


def _mixffn(os_, stats, g_out, h, layer, attn_gain, w_out, mix_post, ffn_pre, w_gate, w_up, w_down, ffn_post):
    S = h.shape[0]
    tm = ROW_TILE
    expand = np.zeros((LANES, ATTN_WIDTH), np.float32)
    for hd in range(ATTN_HEADS):
        expand[hd, hd * HEAD_DIM:(hd + 1) * HEAD_DIM] = 1.0
    resident = 2 * (MIX_WIDTH * D_MODEL + 3 * D_MODEL * D_FF) + 2 * (ATTN_SLABS + len(DILATIONS)) * tm * LANES * 4
    streamed = tm * (len(DILATIONS) * (ATTN_WIDTH + LANES) * 4 + GLA_V_WIDTH * 2 + 2 * D_MODEL * 4)
    return pl.pallas_call(
        functools.partial(_mixffn_kernel, tm=tm),
        grid=(S // tm,),
        in_specs=([_rows(tm // d, d * ATTN_WIDTH) for d in DILATIONS] + [_rows(tm // d, d * LANES) for d in DILATIONS]
                  + [_rows(tm, GLA_V_WIDTH), _rows(tm, D_MODEL), _resident((LANES, ATTN_WIDTH)),
                     _layer_block(layer, (1, ATTN_WIDTH)),
                     pl.BlockSpec((ATTN_WIDTH, D_MODEL), lambda i: (0, 0), pipeline_mode=pl.Buffered(1)),
                     pl.BlockSpec((GLA_V_WIDTH, D_MODEL), lambda i: (ATTN_WIDTH // GLA_V_WIDTH, 0),
                                  pipeline_mode=pl.Buffered(1)),
                     _layer_block(layer, (1, D_MODEL)), _layer_block(layer, (1, D_MODEL)),
                     _resident((D_MODEL, D_FF)), _resident((D_MODEL, D_FF)),
                     _resident((D_FF, D_MODEL)), _layer_block(layer, (1, D_MODEL))]),
        out_specs=_rows(tm, D_MODEL),
        out_shape=jax.ShapeDtypeStruct((S, D_MODEL), F32),
        scratch_shapes=[pltpu.VMEM((2, ATTN_SLABS, tm, LANES), F32),
                        pltpu.VMEM((len(DILATIONS), 2, 1, tm, LANES), F32)],
        compiler_params=pltpu.CompilerParams(dimension_semantics=("arbitrary",),
                                             vmem_limit_bytes=_vmem_limit(resident, streamed)),
        name="mixffn",
    )(*os_, *stats, g_out, h, jnp.asarray(expand, BF16), attn_gain, w_out, w_out, mix_post, ffn_pre,
      w_gate, w_up, w_down, ffn_post)


def _rope_inv_freq():
    inv = np.zeros((1, LANES), np.float32)
    freqs = np.asarray(ROPE_THETA, np.float32) ** (-(np.arange(ROPE_HALF, dtype=np.float32) * 2.0) / ROPE_DIM)
    for lane in range(LANES):
        if lane % HEAD_DIM < ROPE_DIM:
            inv[0, lane] = freqs[lane % ROPE_HALF]
    return jnp.asarray(inv)


def kernel(x, mix_pre_norm, mix_post_norm, ffn_pre_norm, ffn_post_norm, w_in, gla_w_gate_up, gla_b_gate,
           gla_out_norm, attn_out_norm, w_out, w_gate, w_up, w_down):
    B, S, D = x.shape
    depth = mix_pre_norm.shape[0]
    assert D == D_MODEL and S % (max(DILATIONS) * ATTN_SUB) == 0
    assert S % GLA_BLOCK == 0 and S % ROW_TILE == 0 and S % INPROJ_TILE == 0
    assert DILATIONS[0] == 1 and all(b % a == 0 for a, b in zip(DILATIONS, DILATIONS[1:]))
    inv_freq = _rope_inv_freq()
    rope_offsets = _rope_offsets(inv_freq, INPROJ_TILE)
    gains = lambda t: t.reshape(depth, 1, -1)
    mix_pre, mix_post, ffn_pre, ffn_post = gains(mix_pre_norm), gains(mix_post_norm), gains(ffn_pre_norm), gains(ffn_post_norm)
    gla_norm, attn_norm, b_gate = gains(gla_out_norm), gains(attn_out_norm), gains(gla_b_gate)
    w_in_b = jnp.swapaxes(w_in, 1, 2).astype(BF16)
    w_low = jnp.pad(w_in_b[:, MAIN_WIDTH:, :], ((0, 0), (0, LANES - GLA_GATE_RANK), (0, 0)))
    w_gu = jnp.pad(gla_w_gate_up, ((0, 0), (0, LANES - GLA_GATE_RANK), (0, 0))).astype(BF16)

    outs = []
    for b in range(B):
        h = x[b]
        for l in range(depth):
            res = _inproj(h, l, mix_pre, w_in_b, w_low, w_gu, b_gate, inv_freq, rope_offsets)
            n_d = len(DILATIONS)
            qs, ks, vs = res[0:n_d], res[n_d:2 * n_d], res[2 * n_d:3 * n_d]
            gq, gk, gv, gr, glog = res[3 * n_d:]
            pats = [_dilattn(qs[i], ks[i], vs[i], d) for i, d in enumerate(DILATIONS)]
            g_out, w_out_b, w_gate_b, w_up_b, w_down_b = _gla(gq, gk, gv, glog, gr, l, gla_norm,
                                                               (w_out, w_gate, w_up, w_down))
            h = _mixffn([p[0] for p in pats], [p[1] for p in pats], g_out, h, l, attn_norm, w_out_b, mix_post,
                        ffn_pre, w_gate_b, w_up_b, w_down_b, ffn_post)
        outs.append(h)
    return jnp.stack(outs, axis=0)
```

```python
import functools

import numpy as np
import jax
import jax.numpy as jnp
from jax import lax
from jax.experimental import pallas as pl
from jax.experimental.pallas import tpu as pltpu

F32 = jnp.float32
BF16 = jnp.bfloat16

D_MODEL = 1024
HEAD_DIM = 64
ATTN_HEADS = 8
ATTN_WIDTH = ATTN_HEADS * HEAD_DIM
GLA_HEADS = 4
GLA_DK = 64
GLA_DV = 128
GLA_K_WIDTH = GLA_HEADS * GLA_DK
GLA_V_WIDTH = GLA_HEADS * GLA_DV
GLA_GATE_RANK = 16
GLA_TAU = 16.0
MIX_WIDTH = ATTN_WIDTH + GLA_V_WIDTH
D_FF = 2816
ROPE_THETA = 500000.0
ROPE_DIM = HEAD_DIM // 4
ROPE_HALF = ROPE_DIM // 2
DILATIONS = (1, 4, 16)
WINDOW_BACK = 128
RMS_EPS = 1e-6
MAIN_WIDTH = 3 * ATTN_WIDTH + 2 * GLA_K_WIDTH + 2 * GLA_V_WIDTH
GLA_PACK_WIDTH = 2 * GLA_K_WIDTH + 2 * GLA_V_WIDTH

LANES = 128
MXU_DIM = 256
VMEM_BYTES_V7X = 64 * 1024 * 1024
VMEM_REQUEST_FLOOR = 52 * 1024 * 1024

ROW_TILE = 512
INPROJ_TILE = 1024
ATTN_Q_TILE = 2048
ATTN_SUB = WINDOW_BACK
HEADS_PER_GROUP = MXU_DIM // HEAD_DIM
ATTN_GROUPS = ATTN_WIDTH // MXU_DIM
ATTN_SLABS = ATTN_WIDTH // LANES
STAT_SUM_LANE = ATTN_HEADS
GLA_CHUNK = 128
GLA_BLOCK = 1024
GLA_ROW_SPLIT_MIN = 8
FF_CHUNK = 256
NEG_BIG = -1e30
MERGE_MAX_STRIDE = 4
MERGE_COEF_PIECES = 1
GLA_DECAY_PIECES = 2
LOG2_E = 1.4426950408889634


def _dot(a, b):
    return jnp.dot(a, b, preferred_element_type=F32)


def _dot_nt(a, b):
    return lax.dot_general(a, b, (((1,), (1,)), ((), ())), preferred_element_type=F32)


def _dot_tn(a, b):
    return lax.dot_general(a, b, (((0,), (0,)), ((), ())), preferred_element_type=F32)


def _rms(x, gain):
    return x * lax.rsqrt(jnp.mean(x * x, axis=-1, keepdims=True) + RMS_EPS) * gain


def _split3(x):
    hi = x.astype(BF16)
    r1 = x - hi.astype(F32)
    mid = r1.astype(BF16)
    lo = (r1 - mid.astype(F32)).astype(BF16)
    return hi, mid, lo


def _vmem_limit(resident_bytes, streamed_bytes):
    need = 2 * (resident_bytes + 2 * streamed_bytes)
    return int(min(max(need, VMEM_REQUEST_FLOOR), VMEM_BYTES_V7X - 8 * 1024 * 1024))


def _resident(shape):
    nd = len(shape)
    return pl.BlockSpec(shape, lambda *_: (0,) * nd, pipeline_mode=pl.Buffered(1))


def _layer_block(layer, shape, index=None):
    index = (0,) * len(shape) if index is None else index
    return pl.BlockSpec((None,) + tuple(shape), lambda *_: (layer,) + tuple(index), pipeline_mode=pl.Buffered(1))


def _rows(tm, width):
    return pl.BlockSpec((tm, width), lambda i: (i, 0))


def _rope_offsets_kernel(inv_ref, rot_ref):
    off = lax.broadcasted_iota(jnp.int32, rot_ref.shape[1:], 0).astype(F32) * inv_ref[...]
    rot_ref[0] = jnp.cos(off)
    rot_ref[1] = jnp.sin(off)


def _rope_offsets(inv_freq, tm):
    return pl.pallas_call(_rope_offsets_kernel, out_shape=jax.ShapeDtypeStruct((2, tm, LANES), F32),
                          name="ropeoffsets")(inv_freq)


def _inproj_kernel(x_ref, gain_ref, w_ref, wlow_ref, wup_ref, bias_ref, inv_ref, rot_ref, *refs, tm):
    qkv_refs = refs[:len(DILATIONS)]
    gla_ref, glog_ref, slab_ref = refs[len(DILATIONS):]
    xb = _rms(x_ref[...], gain_ref[...]).astype(BF16)

    def proj(c0, width):
        return _dot_nt(xb, w_ref[c0:c0 + width, :])

    base = (pl.program_id(0) * tm).astype(F32) * inv_ref[...]
    cos_b, sin_b = jnp.cos(base), jnp.sin(base)
    cos = cos_b * rot_ref[0] - sin_b * rot_ref[1]
    sin = sin_b * rot_ref[0] + cos_b * rot_ref[1]
    lane = lax.broadcasted_iota(jnp.int32, (tm, LANES), 1) % HEAD_DIM
    cosf = jnp.concatenate([cos] * ATTN_SLABS, axis=1)
    sin_first = jnp.concatenate([jnp.where(lane < ROPE_HALF, -sin, 0.0)] * ATTN_SLABS, axis=1)
    sin_second = jnp.concatenate([jnp.where(lane >= ROPE_HALF, sin, 0.0)] * ATTN_SLABS, axis=1)

    def rope(t):
        return (t * cosf + pltpu.roll(t, ATTN_WIDTH - ROPE_HALF, 1) * sin_first
                + pltpu.roll(t, ROPE_HALF, 1) * sin_second)

    def emit_layouts(t, which):
        for s in range(ATTN_SLABS):
            slab_ref[0, s] = t[:, s * LANES:(s + 1) * LANES]
        src, d_prev = 0, 1
        for d, out_ref in zip(DILATIONS, qkv_refs):
            base = which * d * ATTN_WIDTH
            if d == 1:
                out_ref[:, base:base + ATTN_WIDTH] = t.astype(BF16)
                continue
            n, n_prev, ratio = tm // d, tm // d_prev, d // d_prev
            keep = d != DILATIONS[-1]
            for r in range(d):
                start = (r % d_prev) * n_prev + r // d_prev
                part = jnp.concatenate([slab_ref[src, s, pl.ds(start, n, stride=ratio), :]
                                        for s in range(ATTN_SLABS)], axis=1)
                out_ref[:, base + r * ATTN_WIDTH:base + (r + 1) * ATTN_WIDTH] = part.astype(BF16)
                if keep:
                    for s in range(ATTN_SLABS):
                        slab_ref[1 - src, s, r * n:(r + 1) * n, :] = part[:, s * LANES:(s + 1) * LANES]
            src, d_prev = 1 - src, d

    g_low = _dot_nt(xb, wlow_ref[...])
    z = _dot(g_low.astype(BF16), wup_ref[...]) + bias_ref[...]
    log_sig = jnp.minimum(z, 0.0) - jnp.log1p(jnp.exp(-jnp.abs(z)))
    glog_ref[...] = log_sig * (LOG2_E / GLA_TAU)

    c0 = 0
    emit_layouts(rope(proj(c0, ATTN_WIDTH)) * (HEAD_DIM ** -0.5 * LOG2_E), 0)
    c0 += ATTN_WIDTH
    emit_layouts(rope(proj(c0, ATTN_WIDTH)), 1)
    c0 += ATTN_WIDTH
    emit_layouts(proj(c0, ATTN_WIDTH), 2)
    c0 += ATTN_WIDTH
    g0 = c0
    gla_ref[:, c0 - g0:c0 - g0 + GLA_K_WIDTH] = (proj(c0, GLA_K_WIDTH) * (GLA_DK ** -0.5)).astype(BF16)
    c0 += GLA_K_WIDTH
    gla_ref[:, c0 - g0:c0 - g0 + GLA_K_WIDTH] = proj(c0, GLA_K_WIDTH).astype(BF16)
    c0 += GLA_K_WIDTH
    gla_ref[:, c0 - g0:c0 - g0 + GLA_V_WIDTH] = proj(c0, GLA_V_WIDTH).astype(BF16)
    c0 += GLA_V_WIDTH
    gla_ref[:, c0 - g0:c0 - g0 + GLA_V_WIDTH] = proj(c0, GLA_V_WIDTH).astype(BF16)


def _inproj(h, layer, gain, w_in, w_low, w_up, bias, inv_freq, rope_offsets):
    S = h.shape[0]
    tm = rope_offsets.shape[1]
    qkv_specs = [_rows(tm // d, 3 * d * ATTN_WIDTH) for d in DILATIONS]
    qkv_shapes = [jax.ShapeDtypeStruct((S // d, 3 * d * ATTN_WIDTH), BF16) for d in DILATIONS]
    out_shape = qkv_shapes + [jax.ShapeDtypeStruct((S, GLA_PACK_WIDTH), BF16), jax.ShapeDtypeStruct((S, GLA_K_WIDTH), F32)]
    resident = 2 * (D_MODEL * MAIN_WIDTH + 2 * D_MODEL * LANES)
    streamed = tm * (D_MODEL * 4 + 3 * len(DILATIONS) * ATTN_WIDTH * 2 + MAIN_WIDTH * 2) + tm * MAIN_WIDTH * 4
    return pl.pallas_call(
        functools.partial(_inproj_kernel, tm=tm),
        grid=(S // tm,),
        in_specs=[_rows(tm, D_MODEL), _layer_block(layer, (1, D_MODEL)), _layer_block(layer, (MAIN_WIDTH, D_MODEL)),
                  _layer_block(layer, (LANES, D_MODEL)), _layer_block(layer, (LANES, GLA_K_WIDTH)),
                  _layer_block(layer, (1, GLA_K_WIDTH)), _resident((1, LANES)), _resident((2, tm, LANES))],
        out_specs=qkv_specs + [_rows(tm, GLA_PACK_WIDTH), _rows(tm, GLA_K_WIDTH)],
        out_shape=out_shape,
        scratch_shapes=[pltpu.VMEM((2, ATTN_SLABS, tm, LANES), F32)],
        compiler_params=pltpu.CompilerParams(dimension_semantics=("arbitrary",),
                                             vmem_limit_bytes=_vmem_limit(resident, streamed)),
        name="inproj",
    )(h, gain, w_in, w_low, w_up, bias, inv_freq, rope_offsets)


def _dilattn_kernel(q_ref, kc_ref, kp_ref, vc_ref, vp_ref, o_ref, stat_ref, *, tq, seqs):
    first_tile = pl.program_id(1) == 0
    sub = ATTN_SUB
    grp = MXU_DIM
    a = lax.broadcasted_iota(jnp.int32, (sub, 2 * sub), 0)
    c = lax.broadcasted_iota(jnp.int32, (sub, 2 * sub), 1)
    band = (c >= a) & (c <= a + WINDOW_BACK)
    band_first = band & ((c >= sub) | jnp.logical_not(first_tile))
    q_head = lax.broadcasted_iota(jnp.int32, (sub, grp), 1) // HEAD_DIM

    for seq in range(seqs):
        for j in range(tq // sub):
            rows = slice(j * sub, (j + 1) * sub)
            valid = jnp.concatenate([band_first if j == 0 else band] * HEADS_PER_GROUP, axis=0)
            for g in range(ATTN_GROUPS):
                cols = slice(seq * ATTN_WIDTH + g * grp, seq * ATTN_WIDTH + (g + 1) * grp)
                qg = q_ref[rows, cols]
                if j == 0:
                    kg = jnp.concatenate([kp_ref[:, cols], kc_ref[0:sub, cols]], axis=0)
                    vg = jnp.concatenate([vp_ref[:, cols], vc_ref[0:sub, cols]], axis=0)
                else:
                    kg = kc_ref[(j - 1) * sub:(j + 1) * sub, cols]
                    vg = vc_ref[(j - 1) * sub:(j + 1) * sub, cols]
                qs = jnp.concatenate([jnp.where(q_head == hh, qg, jnp.zeros_like(qg))
                                      for hh in range(HEADS_PER_GROUP)], axis=0)
                s = jnp.where(valid, _dot_nt(qs, kg), NEG_BIG)
                m = jnp.max(s, axis=-1, keepdims=True)
                p = jnp.exp2(s - m)
                sums = jnp.sum(p, axis=-1, keepdims=True)
                pv = _dot(p.astype(BF16), vg)
                for hh in range(HEADS_PER_GROUP):
                    hr = slice(hh * sub, (hh + 1) * sub)
                    head = seq * LANES + g * HEADS_PER_GROUP + hh
                    out0 = seq * ATTN_WIDTH + g * grp + hh * HEAD_DIM
                    o_ref[rows, out0:out0 + HEAD_DIM] = pv[hr, hh * HEAD_DIM:(hh + 1) * HEAD_DIM]
                    stat_ref[rows, head:head + 1] = m[hr]
                    stat_ref[rows, STAT_SUM_LANE + head:STAT_SUM_LANE + head + 1] = sums[hr]
            stat_ref[rows, seq * LANES + 2 * ATTN_HEADS:(seq + 1) * LANES] = jnp.zeros((sub, LANES - 2 * ATTN_HEADS), F32)


def _dilattn(qkv, d):
    rows = qkv.shape[0]
    tq = min(ATTN_Q_TILE, rows)
    seqs = min(d, ATTN_Q_TILE // tq)
    W = seqs * ATTN_WIDTH
    per_kind = d // seqs

    def cur(kind):
        return pl.BlockSpec((tq, W), lambda r, i: (i, kind * per_kind + r))

    def prev(kind):
        return pl.BlockSpec((ATTN_SUB, W),
                            lambda r, i: (jnp.maximum(i * (tq // ATTN_SUB) - 1, 0), kind * per_kind + r))

    streamed = tq * W * (3 * 2 + 4) + 2 * ATTN_SUB * W * 2 + tq * seqs * LANES * 4
    return pl.pallas_call(
        functools.partial(_dilattn_kernel, tq=tq, seqs=seqs),
        grid=(d // seqs, rows // tq),
        in_specs=[cur(0), cur(1), prev(1), cur(2), prev(2)],
        out_specs=[pl.BlockSpec((tq, W), lambda r, i: (i, r)), pl.BlockSpec((tq, seqs * LANES), lambda r, i: (i, r))],
        out_shape=[jax.ShapeDtypeStruct((rows, d * ATTN_WIDTH), F32), jax.ShapeDtypeStruct((rows, d * LANES), F32)],
        compiler_params=pltpu.CompilerParams(dimension_semantics=("arbitrary", "arbitrary"),
                                             vmem_limit_bytes=_vmem_limit(0, streamed)),
        name=f"dilattn{d}",
    )(qkv, qkv, qkv, qkv, qkv)


def _merge_patterns(o_refs, st_refs, expand_ref, gain_ref, oslab_ref, stslab_ref, tm):
    def natural_rows(ref, d, width, slab_ref):
        if d == 1:
            return ref[...]
        slabs = width // LANES
        strides = []
        while d > 1:
            step = min(d, MERGE_MAX_STRIDE)
            strides.append(step)
            d //= step
        blocks = [[ref[:, r * width + s * LANES:r * width + (s + 1) * LANES] for s in range(slabs)]
                  for r in range(int(np.prod(strides)))]
        n = tm // len(blocks)
        for p, step in enumerate(strides):
            dst = p % 2
            groups = len(blocks) // step
            for gq in range(groups):
                for j in range(step):
                    for s in range(slabs):
                        slab_ref[dst, s, pl.ds(gq * n * step + j, n, stride=step), :] = blocks[gq + j * groups][s]
            n *= step
            blocks = [[slab_ref[dst, s, gq * n:(gq + 1) * n, :] for s in range(slabs)] for gq in range(groups)]
        return jnp.concatenate(blocks[0], axis=1)

    stats = [natural_rows(st_refs[i], d, LANES, stslab_ref.at[i]) for i, d in enumerate(DILATIONS)]
    lane = lax.broadcasted_iota(jnp.int32, (tm, LANES), 1)
    top = functools.reduce(jnp.maximum, stats)
    scale = [jnp.exp2(st - top) for st in stats]
    total = sum(sc * pltpu.roll(st, LANES - STAT_SUM_LANE, 1) for sc, st in zip(scale, stats))
    inv_total = jnp.where(lane < ATTN_HEADS, 1.0 / total, 0.0)
    acc = None
    for i, d in enumerate(DILATIONS):
        parts = _split3(scale[i] * inv_total)[:MERGE_COEF_PIECES]
        coef = sum(_dot(p, expand_ref[...]) for p in parts)
        term = coef * natural_rows(o_refs[i], d, ATTN_WIDTH, oslab_ref)
        acc = term if acc is None else acc + term
    return _rms(acc, gain_ref[...]).astype(BF16)


def _gla_tables(C):
    levels = int(np.log2(C))
    t = np.arange(C)[:, None]
    j = np.arange(C)[None, :]
    full, compact = [], []
    for l in range(levels):
        c = 1 << l
        second = (t % (2 * c)) >= c
        mask = (t // (2 * c) == j // (2 * c)) & second & ((j % (2 * c)) < c)
        if c < GLA_ROW_SPLIT_MIN:
            full.append(mask)
        else:
            compact.append(mask[second[:, 0]])
    full.append(t == j)
    seg = (j <= t).astype(np.float32)
    tiled = lambda ms: np.stack([np.tile(mm, (GLA_HEADS, 1)) for mm in ms]).astype(np.float32)
    return seg, tiled(full), tiled(compact)


def _second_half_rows(C, c):
    return [(start, c) for start in range(c, C, 2 * c)]


def _gla_kernel(q_ref, k_ref, v_ref, g_ref, r_ref, gain_ref, seg_ref, msk_ref, mskq_ref, *refs, tb):
    n_cast = (len(refs) - 2) // 2
    cast_in_refs, o_ref, cast_out_refs, state_ref = refs[:n_cast], refs[n_cast], refs[n_cast + 1:-1], refs[-1]
    for src_ref, dst_ref in zip(cast_in_refs, cast_out_refs):
        dst_ref[...] = src_ref[...].astype(BF16)
    C = GLA_CHUNK
    levels = msk_ref.shape[0] - 1 + mskq_ref.shape[0]
    row = lax.broadcasted_iota(jnp.int32, (C, GLA_K_WIDTH), 0)

    @pl.when(pl.program_id(0) == 0)
    def _():
        state_ref[...] = jnp.zeros_like(state_ref)

    def stack_heads(t):
        keep = lax.broadcasted_iota(jnp.int32, t.shape, 1) // GLA_DK
        return jnp.concatenate([jnp.where(keep == hh, t, jnp.zeros_like(t)) for hh in range(GLA_HEADS)], axis=0)

    def add_rows(scores, upd, ranges):
        n = sum(size for _, size in ranges)
        pieces = []
        for hh in range(GLA_HEADS):
            pos, off = 0, 0
            for start, size in ranges:
                if start > pos:
                    pieces.append(scores[hh * C + pos:hh * C + start])
                pieces.append(scores[hh * C + start:hh * C + start + size] + upd[hh * n + off:hh * n + off + size])
                pos, off = start + size, off + size
            if pos < C:
                pieces.append(scores[hh * C + pos:(hh + 1) * C])
        return jnp.concatenate(pieces, axis=0)

    for ci in range(tb // C):
        rows = slice(ci * C, (ci + 1) * C)
        q = q_ref[rows, :].astype(F32)
        k = k_ref[rows, :].astype(F32)
        v = v_ref[rows, :]
        b = sum(_dot(seg_ref[...], part) for part in _split3(g_ref[rows, :])[:GLA_DECAY_PIECES])
        u_from_start = jnp.exp2(b)
        u_to_end = jnp.exp2(b[C - 1:C, :] - b)

        n_small = msk_ref.shape[0] - 1
        scores = msk_ref[n_small] * _dot_nt(stack_heads(q.astype(BF16)), k.astype(BF16))
        block_end = b
        for l in range(levels):
            c = 1 << l
            first_half = (row & c) == 0
            ul = jnp.exp2(jnp.where(first_half, block_end - b, b - pltpu.roll(block_end, c, 0)))
            keys = (k * ul).astype(BF16)
            if l < n_small:
                scores = scores + msk_ref[l] * _dot_nt(stack_heads((q * ul).astype(BF16)), keys)
            else:
                ranges = _second_half_rows(C, c)
                queries = jnp.concatenate([q[st:st + n] * ul[st:st + n] for st, n in ranges], axis=0)
                upd = mskq_ref[l - n_small] * _dot_nt(stack_heads(queries.astype(BF16)), keys)
                scores = add_rows(scores, upd, ranges)
            if l + 1 < levels:
                block_end = jnp.where(first_half, pltpu.roll(block_end, C - c, 0), block_end)

        state_t = state_ref[...]
        inter = _dot_nt(stack_heads((q * u_from_start).astype(BF16)), state_t.astype(BF16))
        scores = scores.astype(BF16)
        gain = gain_ref[...]
        for hh in range(GLA_HEADS):
            hr = slice(hh * C, (hh + 1) * C)
            vc = slice(hh * GLA_DV, (hh + 1) * GLA_DV)
            o_h = _dot(scores[hr], v[:, vc]) + inter[hr]
            o_h = _rms(o_h, gain[:, vc])
            gate = r_ref[rows, vc].astype(F32)
            o_ref[rows, vc] = (o_h * (gate * jax.nn.sigmoid(gate))).astype(BF16)

        upd = _dot_tn(v, (k * u_to_end).astype(BF16))
        decayed = state_t * u_from_start[C - 1:C, :]
        for hh in range(GLA_HEADS):
            lanes = slice(hh * GLA_DK, (hh + 1) * GLA_DK)
            state_ref[:, lanes] = decayed[:, lanes] + upd[hh * GLA_DV:(hh + 1) * GLA_DV, lanes]


def _gla(gpack, glog, layer, gain, later_weights):
    S = gpack.shape[0]
    tb = GLA_BLOCK
    steps = S // tb
    seg, msk, mskq = _gla_tables(GLA_CHUNK)
    cast_in_specs, cast_out_specs, cast_shapes, cast_bytes = [], [], [], 0
    for w in later_weights:
        _, rows, cols = w.shape
        assert rows % (steps * 16) == 0
        share = rows // steps
        cast_in_specs.append(pl.BlockSpec((None, share, cols), lambda i: (layer, i, 0)))
        cast_out_specs.append(_rows(share, cols))
        cast_shapes.append(jax.ShapeDtypeStruct((rows, cols), BF16))
        cast_bytes += share * cols * (4 + 2)
    resident = seg.size * 2 + (msk.size + mskq.size) * 4
    streamed = tb * (2 * GLA_K_WIDTH * 2 + 2 * GLA_V_WIDTH * 2 + GLA_K_WIDTH * 4 + GLA_V_WIDTH * 2) + cast_bytes
    return pl.pallas_call(
        functools.partial(_gla_kernel, tb=tb),
        grid=(steps,),
        in_specs=[pl.BlockSpec((tb, GLA_K_WIDTH), lambda i: (i, 0)), pl.BlockSpec((tb, GLA_K_WIDTH), lambda i: (i, 1)),
                  pl.BlockSpec((tb, GLA_V_WIDTH), lambda i: (i, 2 * GLA_K_WIDTH // GLA_V_WIDTH)),
                  _rows(tb, GLA_K_WIDTH),
                  pl.BlockSpec((tb, GLA_V_WIDTH), lambda i: (i, 2 * GLA_K_WIDTH // GLA_V_WIDTH + 1)),
                  _layer_block(layer, (1, GLA_V_WIDTH)), _resident(seg.shape),
                  _resident(msk.shape), _resident(mskq.shape)] + cast_in_specs,
        out_specs=[_rows(tb, GLA_V_WIDTH)] + cast_out_specs,
        out_shape=[jax.ShapeDtypeStruct((S, GLA_V_WIDTH), BF16)] + cast_shapes,
        scratch_shapes=[pltpu.VMEM((GLA_DV, GLA_K_WIDTH), F32)],
        compiler_params=pltpu.CompilerParams(dimension_semantics=("arbitrary",),
                                             vmem_limit_bytes=_vmem_limit(resident, streamed)),
        name="gla",
    )(gpack, gpack, gpack, glog, gpack, gain, jnp.asarray(seg, BF16), jnp.asarray(msk, F32), jnp.asarray(mskq, F32),
      *later_weights)


def _mixffn_kernel(*refs, tm):
    n_d = len(DILATIONS)
    o_refs, st_refs = refs[:n_d], refs[n_d:2 * n_d]
    (g_ref, h_ref, expand_ref, again_ref, wa_ref, wg_ref, mpost_ref, fpre_ref, wgate_ref, wup_ref, wdown_ref,
     fpost_ref, out_ref, oslab_ref, stslab_ref) = refs[2 * n_d:]
    a_out = _merge_patterns(o_refs, st_refs, expand_ref, again_ref, oslab_ref, stslab_ref, tm)
    m = _dot(a_out, wa_ref[...]) + _dot(g_ref[...], wg_ref[...])
    h = h_ref[...] + _rms(m, mpost_ref[...])
    x = _rms(h, fpre_ref[...]).astype(BF16)
    acc = None
    for c0 in range(0, D_FF, FF_CHUNK):
        cols = slice(c0, c0 + FF_CHUNK)
        gate = _dot(x, wgate_ref[:, cols])
        act = (gate * jax.nn.sigmoid(gate) * _dot(x, wup_ref[:, cols])).astype(BF16)
        part = _dot(act, wdown_ref[cols, :])
        acc = part if acc is None else acc + part
    out_ref[...] = h + _rms(acc, fpost_ref[...])


def _mixffn(os_, stats, g_out, h, layer, attn_gain, w_out, mix_post, ffn_pre, w_gate, w_up, w_down, ffn_post):
    S = h.shape[0]
    tm = ROW_TILE
    expand = np.zeros((LANES, ATTN_WIDTH), np.float32)
    for hd in range(ATTN_HEADS):
        expand[hd, hd * HEAD_DIM:(hd + 1) * HEAD_DIM] = 1.0
    resident = 2 * (MIX_WIDTH * D_MODEL + 3 * D_MODEL * D_FF) + 2 * (ATTN_SLABS + len(DILATIONS)) * tm * LANES * 4
    streamed = tm * (len(DILATIONS) * (ATTN_WIDTH + LANES) * 4 + GLA_V_WIDTH * 2 + 2 * D_MODEL * 4)
    return pl.pallas_call(
        functools.partial(_mixffn_kernel, tm=tm),
        grid=(S // tm,),
        in_specs=([_rows(tm // d, d * ATTN_WIDTH) for d in DILATIONS] + [_rows(tm // d, d * LANES) for d in DILATIONS]
                  + [_rows(tm, GLA_V_WIDTH), _rows(tm, D_MODEL), _resident((LANES, ATTN_WIDTH)),
                     _layer_block(layer, (1, ATTN_WIDTH)),
                     pl.BlockSpec((ATTN_WIDTH, D_MODEL), lambda i: (0, 0), pipeline_mode=pl.Buffered(1)),
                     pl.BlockSpec((GLA_V_WIDTH, D_MODEL), lambda i: (ATTN_WIDTH // GLA_V_WIDTH, 0),
                                  pipeline_mode=pl.Buffered(1)),
                     _layer_block(layer, (1, D_MODEL)), _layer_block(layer, (1, D_MODEL)),
                     _resident((D_MODEL, D_FF)), _resident((D_MODEL, D_FF)),
                     _resident((D_FF, D_MODEL)), _layer_block(layer, (1, D_MODEL))]),
        out_specs=_rows(tm, D_MODEL),
        out_shape=jax.ShapeDtypeStruct((S, D_MODEL), F32),
        scratch_shapes=[pltpu.VMEM((2, ATTN_SLABS, tm, LANES), F32),
                        pltpu.VMEM((len(DILATIONS), 2, 1, tm, LANES), F32)],
        compiler_params=pltpu.CompilerParams(dimension_semantics=("arbitrary",),
                                             vmem_limit_bytes=_vmem_limit(resident, streamed)),
        name="mixffn",
    )(*os_, *stats, g_out, h, jnp.asarray(expand, BF16), attn_gain, w_out, w_out, mix_post, ffn_pre,
      w_gate, w_up, w_down, ffn_post)


def _rope_inv_freq():
    inv = np.zeros((1, LANES), np.float32)
    freqs = np.asarray(ROPE_THETA, np.float32) ** (-(np.arange(ROPE_HALF, dtype=np.float32) * 2.0) / ROPE_DIM)
    for lane in range(LANES):
        if lane % HEAD_DIM < ROPE_DIM:
            inv[0, lane] = freqs[lane % ROPE_HALF]
    return jnp.asarray(inv)


def kernel(x, mix_pre_norm, mix_post_norm, ffn_pre_norm, ffn_post_norm, w_in, gla_w_gate_up, gla_b_gate,
           gla_out_norm, attn_out_norm, w_out, w_gate, w_up, w_down):
    B, S, D = x.shape
    depth = mix_pre_norm.shape[0]
    assert D == D_MODEL and S % (max(DILATIONS) * ATTN_SUB) == 0
    assert S % GLA_BLOCK == 0 and S % ROW_TILE == 0 and S % INPROJ_TILE == 0
    assert DILATIONS[0] == 1 and all(b % a == 0 for a, b in zip(DILATIONS, DILATIONS[1:]))
    inv_freq = _rope_inv_freq()
    rope_offsets = _rope_offsets(inv_freq, INPROJ_TILE)
    gains = lambda t: t.reshape(depth, 1, -1)
    mix_pre, mix_post, ffn_pre, ffn_post = gains(mix_pre_norm), gains(mix_post_norm), gains(ffn_pre_norm), gains(ffn_post_norm)
    gla_norm, attn_norm, b_gate = gains(gla_out_norm), gains(attn_out_norm), gains(gla_b_gate)
    w_in_b = jnp.swapaxes(w_in, 1, 2).astype(BF16)
    w_low = jnp.pad(w_in_b[:, MAIN_WIDTH:, :], ((0, 0), (0, LANES - GLA_GATE_RANK), (0, 0)))
    w_gu = jnp.pad(gla_w_gate_up, ((0, 0), (0, LANES - GLA_GATE_RANK), (0, 0))).astype(BF16)

    outs = []
    for b in range(B):
        h = x[b]
        for l in range(depth):
            *qkvs, gpack, glog = _inproj(h, l, mix_pre, w_in_b, w_low, w_gu, b_gate, inv_freq, rope_offsets)
            pats = [_dilattn(qkv, d) for qkv, d in zip(qkvs, DILATIONS)]
            g_out, w_out_b, w_gate_b, w_up_b, w_down_b = _gla(gpack, glog, l, gla_norm,
                                                               (w_out, w_gate, w_up, w_down))
            h = _mixffn([p[0] for p in pats], [p[1] for p in pats], g_out, h, l, attn_norm, w_out_b, mix_post,
                        ffn_pre, w_gate_b, w_up_b, w_down_b, ffn_post)
        outs.append(h)
    return jnp.stack(outs, axis=0)
```

```python
import functools

import numpy as np
import jax
import jax.numpy as jnp
from jax import lax
from jax.experimental import pallas as pl
from jax.experimental.pallas import tpu as pltpu

F32 = jnp.float32
BF16 = jnp.bfloat16

D_MODEL = 1024
HEAD_DIM = 64
ATTN_HEADS = 8
ATTN_WIDTH = ATTN_HEADS * HEAD_DIM
GLA_HEADS = 4
GLA_DK = 64
GLA_DV = 128
GLA_K_WIDTH = GLA_HEADS * GLA_DK
GLA_V_WIDTH = GLA_HEADS * GLA_DV
GLA_GATE_RANK = 16
GLA_TAU = 16.0
MIX_WIDTH = ATTN_WIDTH + GLA_V_WIDTH
D_FF = 2816
ROPE_THETA = 500000.0
ROPE_DIM = HEAD_DIM // 4
ROPE_HALF = ROPE_DIM // 2
DILATIONS = (1, 4, 16)
WINDOW_BACK = 128
RMS_EPS = 1e-6
MAIN_WIDTH = 3 * ATTN_WIDTH + 2 * GLA_K_WIDTH + 2 * GLA_V_WIDTH
GLA_PACK_WIDTH = 2 * GLA_K_WIDTH + 2 * GLA_V_WIDTH

LANES = 128
MXU_DIM = 256
VMEM_BYTES_V7X = 64 * 1024 * 1024
VMEM_REQUEST_FLOOR = 52 * 1024 * 1024

ROW_TILE = 512
INPROJ_TILE = 1024
ATTN_Q_TILE = 2048
ATTN_SUB = WINDOW_BACK
HEADS_PER_GROUP = MXU_DIM // HEAD_DIM
ATTN_GROUPS = ATTN_WIDTH // MXU_DIM
ATTN_SLABS = ATTN_WIDTH // LANES
STAT_SUM_LANE = ATTN_HEADS
GLA_CHUNK = 128
GLA_BLOCK = 1024
GLA_ROW_SPLIT_MIN = 8
FF_CHUNK = 256
NEG_BIG = -1e30
MERGE_MAX_STRIDE = 4
MERGE_COEF_PIECES = 1
GLA_DECAY_PIECES = 2
LOG2_E = 1.4426950408889634


def _dot(a, b):
    return jnp.dot(a, b, preferred_element_type=F32)


def _dot_nt(a, b):
    return lax.dot_general(a, b, (((1,), (1,)), ((), ())), preferred_element_type=F32)


def _dot_tn(a, b):
    return lax.dot_general(a, b, (((0,), (0,)), ((), ())), preferred_element_type=F32)


def _rms(x, gain):
    return x * lax.rsqrt(jnp.mean(x * x, axis=-1, keepdims=True) + RMS_EPS) * gain


def _split3(x):
    hi = x.astype(BF16)
    r1 = x - hi.astype(F32)
    mid = r1.astype(BF16)
    lo = (r1 - mid.astype(F32)).astype(BF16)
    return hi, mid, lo


def _vmem_limit(resident_bytes, streamed_bytes):
    need = 2 * (resident_bytes + 2 * streamed_bytes)
    return int(min(max(need, VMEM_REQUEST_FLOOR), VMEM_BYTES_V7X - 8 * 1024 * 1024))


def _resident(shape):
    nd = len(shape)
    return pl.BlockSpec(shape, lambda *_: (0,) * nd, pipeline_mode=pl.Buffered(1))


def _layer_block(layer, shape, index=None):
    index = (0,) * len(shape) if index is None else index
    return pl.BlockSpec((None,) + tuple(shape), lambda *_: (layer,) + tuple(index), pipeline_mode=pl.Buffered(1))


def _rows(tm, width):
    return pl.BlockSpec((tm, width), lambda i: (i, 0))


def _rope_offsets(inv_freq, tm):
    off = (np.arange(tm, dtype=np.float32)[:, None] * inv_freq).astype(np.float64)
    return np.stack([np.cos(off), np.sin(off)]).astype(np.float32)


def _inproj_kernel(x_ref, gain_ref, w_ref, wup_ref, bias_ref, inv_ref, rot_ref, *refs, tm, layer):
    qkv_refs = refs[:len(DILATIONS)]
    gla_ref, glog_ref, slab_ref = refs[len(DILATIONS):]
    xb = _rms(x_ref[...], gain_ref[layer:layer + 1, :]).astype(BF16)

    def proj(c0, width):
        return _dot_nt(xb, w_ref[c0:c0 + width, :])

    base = (pl.program_id(0) * tm).astype(F32) * inv_ref[...]
    cos_b, sin_b = jnp.cos(base), jnp.sin(base)
    cos = cos_b * rot_ref[0] - sin_b * rot_ref[1]
    sin = sin_b * rot_ref[0] + cos_b * rot_ref[1]
    lane = lax.broadcasted_iota(jnp.int32, (tm, LANES), 1) % HEAD_DIM
    cosf = jnp.concatenate([cos] * ATTN_SLABS, axis=1)
    sin_first = jnp.concatenate([jnp.where(lane < ROPE_HALF, -sin, 0.0)] * ATTN_SLABS, axis=1)
    sin_second = jnp.concatenate([jnp.where(lane >= ROPE_HALF, sin, 0.0)] * ATTN_SLABS, axis=1)

    def rope(t):
        return (t * cosf + pltpu.roll(t, ATTN_WIDTH - ROPE_HALF, 1) * sin_first
                + pltpu.roll(t, ROPE_HALF, 1) * sin_second)

    def emit_layouts(t, which):
        for s in range(ATTN_SLABS):
            slab_ref[0, s] = t[:, s * LANES:(s + 1) * LANES]
        src, d_prev = 0, 1
        for d, out_ref in zip(DILATIONS, qkv_refs):
            base = which * d * ATTN_WIDTH
            if d == 1:
                out_ref[:, base:base + ATTN_WIDTH] = t.astype(BF16)
                continue
            n, n_prev, ratio = tm // d, tm // d_prev, d // d_prev
            keep = d != DILATIONS[-1]
            for r in range(d):
                start = (r % d_prev) * n_prev + r // d_prev
                part = jnp.concatenate([slab_ref[src, s, pl.ds(start, n, stride=ratio), :]
                                        for s in range(ATTN_SLABS)], axis=1)
                out_ref[:, base + r * ATTN_WIDTH:base + (r + 1) * ATTN_WIDTH] = part.astype(BF16)
                if keep:
                    for s in range(ATTN_SLABS):
                        slab_ref[1 - src, s, r * n:(r + 1) * n, :] = part[:, s * LANES:(s + 1) * LANES]
            src, d_prev = 1 - src, d

    rank_pad = LANES - GLA_GATE_RANK
    w_low = jnp.concatenate([w_ref[MAIN_WIDTH:, :], jnp.zeros((rank_pad, D_MODEL), BF16)], axis=0)
    w_up = jnp.concatenate([wup_ref[...].astype(BF16), jnp.zeros((rank_pad, GLA_K_WIDTH), BF16)], axis=0)
    g_low = _dot_nt(xb, w_low)
    z = _dot(g_low.astype(BF16), w_up) + bias_ref[layer:layer + 1, :]
    log_sig = jnp.minimum(z, 0.0) - jnp.log1p(jnp.exp(-jnp.abs(z)))
    glog_ref[...] = log_sig * (LOG2_E / GLA_TAU)

    c0 = 0
    emit_layouts(rope(proj(c0, ATTN_WIDTH)) * (HEAD_DIM ** -0.5 * LOG2_E), 0)
    c0 += ATTN_WIDTH
    emit_layouts(rope(proj(c0, ATTN_WIDTH)), 1)
    c0 += ATTN_WIDTH
    emit_layouts(proj(c0, ATTN_WIDTH), 2)
    c0 += ATTN_WIDTH
    g0 = c0
    gla_ref[:, c0 - g0:c0 - g0 + GLA_K_WIDTH] = (proj(c0, GLA_K_WIDTH) * (GLA_DK ** -0.5)).astype(BF16)
    c0 += GLA_K_WIDTH
    gla_ref[:, c0 - g0:c0 - g0 + GLA_K_WIDTH] = proj(c0, GLA_K_WIDTH).astype(BF16)
    c0 += GLA_K_WIDTH
    gla_ref[:, c0 - g0:c0 - g0 + GLA_V_WIDTH] = proj(c0, GLA_V_WIDTH).astype(BF16)
    c0 += GLA_V_WIDTH
    gla_ref[:, c0 - g0:c0 - g0 + GLA_V_WIDTH] = proj(c0, GLA_V_WIDTH).astype(BF16)


def _inproj(h, layer, gain, w_in, w_up, bias, inv_freq, rope_offsets):
    S = h.shape[0]
    tm = rope_offsets.shape[1]
    in_width = MAIN_WIDTH + GLA_GATE_RANK
    qkv_specs = [_rows(tm // d, 3 * d * ATTN_WIDTH) for d in DILATIONS]
    qkv_shapes = [jax.ShapeDtypeStruct((S // d, 3 * d * ATTN_WIDTH), BF16) for d in DILATIONS]
    out_shape = qkv_shapes + [jax.ShapeDtypeStruct((S, GLA_PACK_WIDTH), BF16), jax.ShapeDtypeStruct((S, GLA_K_WIDTH), F32)]
    resident = 2 * (D_MODEL * in_width + 2 * D_MODEL * LANES) + 2 * tm * LANES * 4
    streamed = tm * (D_MODEL * 4 + 3 * len(DILATIONS) * ATTN_WIDTH * 2 + MAIN_WIDTH * 2) + tm * MAIN_WIDTH * 4
    return pl.pallas_call(
        functools.partial(_inproj_kernel, tm=tm, layer=layer),
        grid=(S // tm,),
        in_specs=[_rows(tm, D_MODEL), _resident(gain.shape), _layer_block(layer, (in_width, D_MODEL)),
                  _layer_block(layer, (GLA_GATE_RANK, GLA_K_WIDTH)), _resident(bias.shape),
                  _resident((1, LANES)), _resident((2, tm, LANES))],
        out_specs=qkv_specs + [_rows(tm, GLA_PACK_WIDTH), _rows(tm, GLA_K_WIDTH)],
        out_shape=out_shape,
        scratch_shapes=[pltpu.VMEM((2, ATTN_SLABS, tm, LANES), F32)],
        compiler_params=pltpu.CompilerParams(dimension_semantics=("arbitrary",),
                                             vmem_limit_bytes=_vmem_limit(resident, streamed)),
        name="inproj",
    )(h, gain, w_in, w_up, bias, jnp.asarray(inv_freq), jnp.asarray(rope_offsets))


def _dilattn_kernel(q_ref, kc_ref, kp_ref, vc_ref, vp_ref, o_ref, stat_ref, *, tq, seqs):
    first_tile = pl.program_id(1) == 0
    sub = ATTN_SUB
    grp = MXU_DIM
    a = lax.broadcasted_iota(jnp.int32, (sub, 2 * sub), 0)
    c = lax.broadcasted_iota(jnp.int32, (sub, 2 * sub), 1)
    band = (c >= a) & (c <= a + WINDOW_BACK)
    band_first = band & ((c >= sub) | jnp.logical_not(first_tile))
    q_head = lax.broadcasted_iota(jnp.int32, (sub, grp), 1) // HEAD_DIM

    for seq in range(seqs):
        for j in range(tq // sub):
            rows = slice(j * sub, (j + 1) * sub)
            valid = jnp.concatenate([band_first if j == 0 else band] * HEADS_PER_GROUP, axis=0)
            for g in range(ATTN_GROUPS):
                cols = slice(seq * ATTN_WIDTH + g * grp, seq * ATTN_WIDTH + (g + 1) * grp)
                qg = q_ref[rows, cols]
                if j == 0:
                    kg = jnp.concatenate([kp_ref[:, cols], kc_ref[0:sub, cols]], axis=0)
                    vg = jnp.concatenate([vp_ref[:, cols], vc_ref[0:sub, cols]], axis=0)
                else:
                    kg = kc_ref[(j - 1) * sub:(j + 1) * sub, cols]
                    vg = vc_ref[(j - 1) * sub:(j + 1) * sub, cols]
                qs = jnp.concatenate([jnp.where(q_head == hh, qg, jnp.zeros_like(qg))
                                      for hh in range(HEADS_PER_GROUP)], axis=0)
                s = jnp.where(valid, _dot_nt(qs, kg), NEG_BIG)
                m = jnp.max(s, axis=-1, keepdims=True)
                p = jnp.exp2(s - m)
                sums = jnp.sum(p, axis=-1, keepdims=True)
                pv = _dot(p.astype(BF16), vg)
                for hh in range(HEADS_PER_GROUP):
                    hr = slice(hh * sub, (hh + 1) * sub)
                    head = seq * LANES + g * HEADS_PER_GROUP + hh
                    out0 = seq * ATTN_WIDTH + g * grp + hh * HEAD_DIM
                    o_ref[rows, out0:out0 + HEAD_DIM] = pv[hr, hh * HEAD_DIM:(hh + 1) * HEAD_DIM]
                    stat_ref[rows, head:head + 1] = m[hr]
                    stat_ref[rows, STAT_SUM_LANE + head:STAT_SUM_LANE + head + 1] = sums[hr]
            stat_ref[rows, seq * LANES + 2 * ATTN_HEADS:(seq + 1) * LANES] = jnp.zeros((sub, LANES - 2 * ATTN_HEADS), F32)


def _dilattn(qkv, d):
    rows = qkv.shape[0]
    tq = min(ATTN_Q_TILE, rows)
    seqs = min(d, ATTN_Q_TILE // tq)
    W = seqs * ATTN_WIDTH
    per_kind = d // seqs

    def cur(kind):
        return pl.BlockSpec((tq, W), lambda r, i: (i, kind * per_kind + r))

    def prev(kind):
        return pl.BlockSpec((ATTN_SUB, W),
                            lambda r, i: (jnp.maximum(i * (tq // ATTN_SUB) - 1, 0), kind * per_kind + r))

    streamed = tq * W * (3 * 2 + 4) + 2 * ATTN_SUB * W * 2 + tq * seqs * LANES * 4
    return pl.pallas_call(
        functools.partial(_dilattn_kernel, tq=tq, seqs=seqs),
        grid=(d // seqs, rows // tq),
        in_specs=[cur(0), cur(1), prev(1), cur(2), prev(2)],
        out_specs=[pl.BlockSpec((tq, W), lambda r, i: (i, r)), pl.BlockSpec((tq, seqs * LANES), lambda r, i: (i, r))],
        out_shape=[jax.ShapeDtypeStruct((rows, d * ATTN_WIDTH), F32), jax.ShapeDtypeStruct((rows, d * LANES), F32)],
        compiler_params=pltpu.CompilerParams(dimension_semantics=("arbitrary", "arbitrary"),
                                             vmem_limit_bytes=_vmem_limit(0, streamed)),
        name=f"dilattn{d}",
    )(qkv, qkv, qkv, qkv, qkv)


def _merge_patterns(o_refs, st_refs, expand_ref, gain, oslab_ref, stslab_ref, tm):
    def natural_rows(ref, d, width, slab_ref):
        if d == 1:
            return ref[...]
        slabs = width // LANES
        strides = []
        while d > 1:
            step = min(d, MERGE_MAX_STRIDE)
            strides.append(step)
            d //= step
        blocks = [[ref[:, r * width + s * LANES:r * width + (s + 1) * LANES] for s in range(slabs)]
                  for r in range(int(np.prod(strides)))]
        n = tm // len(blocks)
        for p, step in enumerate(strides):
            dst = p % 2
            groups = len(blocks) // step
            for gq in range(groups):
                for j in range(step):
                    for s in range(slabs):
                        slab_ref[dst, s, pl.ds(gq * n * step + j, n, stride=step), :] = blocks[gq + j * groups][s]
            n *= step
            blocks = [[slab_ref[dst, s, gq * n:(gq + 1) * n, :] for s in range(slabs)] for gq in range(groups)]
        return jnp.concatenate(blocks[0], axis=1)

    stats = [natural_rows(st_refs[i], d, LANES, stslab_ref.at[i]) for i, d in enumerate(DILATIONS)]
    lane = lax.broadcasted_iota(jnp.int32, (tm, LANES), 1)
    top = functools.reduce(jnp.maximum, stats)
    scale = [jnp.exp2(st - top) for st in stats]
    total = sum(sc * pltpu.roll(st, LANES - STAT_SUM_LANE, 1) for sc, st in zip(scale, stats))
    inv_total = jnp.where(lane < ATTN_HEADS, 1.0 / total, 0.0)
    acc = None
    for i, d in enumerate(DILATIONS):
        parts = _split3(scale[i] * inv_total)[:MERGE_COEF_PIECES]
        coef = sum(_dot(p, expand_ref[...]) for p in parts)
        term = coef * natural_rows(o_refs[i], d, ATTN_WIDTH, oslab_ref)
        acc = term if acc is None else acc + term
    return _rms(acc, gain).astype(BF16)


def _gla_tables(C):
    levels = int(np.log2(C))
    t = np.arange(C)[:, None]
    j = np.arange(C)[None, :]
    full, compact = [], []
    for l in range(levels):
        c = 1 << l
        second = (t % (2 * c)) >= c
        mask = (t // (2 * c) == j // (2 * c)) & second & ((j % (2 * c)) < c)
        if c < GLA_ROW_SPLIT_MIN:
            full.append(mask)
        else:
            compact.append(mask[second[:, 0]])
    full.append(t == j)
    seg = (j <= t).astype(np.float32)
    tiled = lambda ms: np.stack([np.tile(mm, (GLA_HEADS, 1)) for mm in ms]).astype(np.float32)
    return seg, tiled(full), tiled(compact)


def _second_half_rows(C, c):
    return [(start, c) for start in range(c, C, 2 * c)]


def _gla_kernel(q_ref, k_ref, v_ref, g_ref, r_ref, gain_ref, seg_ref, msk_ref, mskq_ref, *refs, tb, layer):
    n_cast = (len(refs) - 2) // 2
    cast_in_refs, o_ref, cast_out_refs, state_ref = refs[:n_cast], refs[n_cast], refs[n_cast + 1:-1], refs[-1]
    for src_ref, dst_ref in zip(cast_in_refs, cast_out_refs):
        dst_ref[...] = src_ref[...].astype(BF16)
    C = GLA_CHUNK
    levels = msk_ref.shape[0] - 1 + mskq_ref.shape[0]
    row = lax.broadcasted_iota(jnp.int32, (C, GLA_K_WIDTH), 0)

    @pl.when(pl.program_id(0) == 0)
    def _():
        state_ref[...] = jnp.zeros_like(state_ref)

    def stack_heads(t):
        keep = lax.broadcasted_iota(jnp.int32, t.shape, 1) // GLA_DK
        return jnp.concatenate([jnp.where(keep == hh, t, jnp.zeros_like(t)) for hh in range(GLA_HEADS)], axis=0)

    def add_rows(scores, upd, ranges):
        n = sum(size for _, size in ranges)
        pieces = []
        for hh in range(GLA_HEADS):
            pos, off = 0, 0
            for start, size in ranges:
                if start > pos:
                    pieces.append(scores[hh * C + pos:hh * C + start])
                pieces.append(scores[hh * C + start:hh * C + start + size] + upd[hh * n + off:hh * n + off + size])
                pos, off = start + size, off + size
            if pos < C:
                pieces.append(scores[hh * C + pos:(hh + 1) * C])
        return jnp.concatenate(pieces, axis=0)

    for ci in range(tb // C):
        rows = slice(ci * C, (ci + 1) * C)
        q = q_ref[rows, :].astype(F32)
        k = k_ref[rows, :].astype(F32)
        v = v_ref[rows, :]
        b = sum(_dot(seg_ref[...], part) for part in _split3(g_ref[rows, :])[:GLA_DECAY_PIECES])
        u_from_start = jnp.exp2(b)
        u_to_end = jnp.exp2(b[C - 1:C, :] - b)

        n_small = msk_ref.shape[0] - 1
        scores = msk_ref[n_small] * _dot_nt(stack_heads(q.astype(BF16)), k.astype(BF16))
        block_end = b
        for l in range(levels):
            c = 1 << l
            first_half = (row & c) == 0
            ul = jnp.exp2(jnp.where(first_half, block_end - b, b - pltpu.roll(block_end, c, 0)))
            keys = (k * ul).astype(BF16)
            if l < n_small:
                scores = scores + msk_ref[l] * _dot_nt(stack_heads((q * ul).astype(BF16)), keys)
            else:
                ranges = _second_half_rows(C, c)
                queries = jnp.concatenate([q[st:st + n] * ul[st:st + n] for st, n in ranges], axis=0)
                upd = mskq_ref[l - n_small] * _dot_nt(stack_heads(queries.astype(BF16)), keys)
                scores = add_rows(scores, upd, ranges)
            if l + 1 < levels:
                block_end = jnp.where(first_half, pltpu.roll(block_end, C - c, 0), block_end)

        state_t = state_ref[...]
        inter = _dot_nt(stack_heads((q * u_from_start).astype(BF16)), state_t.astype(BF16))
        scores = scores.astype(BF16)
        gain = gain_ref[layer:layer + 1, :]
        for hh in range(GLA_HEADS):
            hr = slice(hh * C, (hh + 1) * C)
            vc = slice(hh * GLA_DV, (hh + 1) * GLA_DV)
            o_h = _dot(scores[hr], v[:, vc]) + inter[hr]
            o_h = _rms(o_h, gain[:, vc])
            gate = r_ref[rows, vc].astype(F32)
            o_ref[rows, vc] = (o_h * (gate * jax.nn.sigmoid(gate))).astype(BF16)

        upd = _dot_tn(v, (k * u_to_end).astype(BF16))
        decayed = state_t * u_from_start[C - 1:C, :]
        for hh in range(GLA_HEADS):
            lanes = slice(hh * GLA_DK, (hh + 1) * GLA_DK)
            state_ref[:, lanes] = decayed[:, lanes] + upd[hh * GLA_DV:(hh + 1) * GLA_DV, lanes]


def _gla(gpack, glog, layer, gain, later_weights):
    S = gpack.shape[0]
    tb = GLA_BLOCK
    steps = S // tb
    seg, msk, mskq = _gla_tables(GLA_CHUNK)
    cast_in_specs, cast_out_specs, cast_shapes, cast_bytes = [], [], [], 0
    for w in later_weights:
        _, rows, cols = w.shape
        assert rows % (steps * 16) == 0
        share = rows // steps
        cast_in_specs.append(pl.BlockSpec((None, share, cols), lambda i: (layer, i, 0)))
        cast_out_specs.append(_rows(share, cols))
        cast_shapes.append(jax.ShapeDtypeStruct((rows, cols), BF16))
        cast_bytes += share * cols * (4 + 2)
    resident = seg.size * 2 + (msk.size + mskq.size) * 4
    streamed = tb * (2 * GLA_K_WIDTH * 2 + 2 * GLA_V_WIDTH * 2 + GLA_K_WIDTH * 4 + GLA_V_WIDTH * 2) + cast_bytes
    return pl.pallas_call(
        functools.partial(_gla_kernel, tb=tb, layer=layer),
        grid=(steps,),
        in_specs=[pl.BlockSpec((tb, GLA_K_WIDTH), lambda i: (i, 0)), pl.BlockSpec((tb, GLA_K_WIDTH), lambda i: (i, 1)),
                  pl.BlockSpec((tb, GLA_V_WIDTH), lambda i: (i, 2 * GLA_K_WIDTH // GLA_V_WIDTH)),
                  _rows(tb, GLA_K_WIDTH),
                  pl.BlockSpec((tb, GLA_V_WIDTH), lambda i: (i, 2 * GLA_K_WIDTH // GLA_V_WIDTH + 1)),
                  _resident(gain.shape), _resident(seg.shape),
                  _resident(msk.shape), _resident(mskq.shape)] + cast_in_specs,
        out_specs=[_rows(tb, GLA_V_WIDTH)] + cast_out_specs,
        out_shape=[jax.ShapeDtypeStruct((S, GLA_V_WIDTH), BF16)] + cast_shapes,
        scratch_shapes=[pltpu.VMEM((GLA_DV, GLA_K_WIDTH), F32)],
        compiler_params=pltpu.CompilerParams(dimension_semantics=("arbitrary",),
                                             vmem_limit_bytes=_vmem_limit(resident, streamed)),
        name="gla",
    )(gpack, gpack, gpack, glog, gpack, gain, jnp.asarray(seg, BF16), jnp.asarray(msk, F32), jnp.asarray(mskq, F32),
      *later_weights)


def _mixffn_kernel(*refs, tm, layer):
    n_d = len(DILATIONS)
    o_refs, st_refs = refs[:n_d], refs[n_d:2 * n_d]
    (g_ref, h_ref, expand_ref, again_ref, wa_ref, wg_ref, mpost_ref, fpre_ref, wgate_ref, wup_ref, wdown_ref,
     fpost_ref, out_ref, oslab_ref, stslab_ref) = refs[2 * n_d:]
    gain = lambda ref: ref[layer:layer + 1, :]
    a_out = _merge_patterns(o_refs, st_refs, expand_ref, gain(again_ref), oslab_ref, stslab_ref, tm)
    m = _dot(a_out, wa_ref[...]) + _dot(g_ref[...], wg_ref[...])
    h = h_ref[...] + _rms(m, gain(mpost_ref))
    x = _rms(h, gain(fpre_ref)).astype(BF16)
    acc = None
    for c0 in range(0, D_FF, FF_CHUNK):
        cols = slice(c0, c0 + FF_CHUNK)
        gate = _dot(x, wgate_ref[:, cols])
        act = (gate * jax.nn.sigmoid(gate) * _dot(x, wup_ref[:, cols])).astype(BF16)
        part = _dot(act, wdown_ref[cols, :])
        acc = part if acc is None else acc + part
    out_ref[...] = h + _rms(acc, gain(fpost_ref))


def _mixffn(os_, stats, g_out, h, layer, attn_gain, w_out, mix_post, ffn_pre, w_gate, w_up, w_down, ffn_post):
    S = h.shape[0]
    tm = ROW_TILE
    expand = np.zeros((LANES, ATTN_WIDTH), np.float32)
    for hd in range(ATTN_HEADS):
        expand[hd, hd * HEAD_DIM:(hd + 1) * HEAD_DIM] = 1.0
    resident = 2 * (MIX_WIDTH * D_MODEL + 3 * D_MODEL * D_FF) + 2 * (ATTN_SLABS + len(DILATIONS)) * tm * LANES * 4
    streamed = tm * (len(DILATIONS) * (ATTN_WIDTH + LANES) * 4 + GLA_V_WIDTH * 2 + 2 * D_MODEL * 4)
    return pl.pallas_call(
        functools.partial(_mixffn_kernel, tm=tm, layer=layer),
        grid=(S // tm,),
        in_specs=([_rows(tm // d, d * ATTN_WIDTH) for d in DILATIONS] + [_rows(tm // d, d * LANES) for d in DILATIONS]
                  + [_rows(tm, GLA_V_WIDTH), _rows(tm, D_MODEL), _resident((LANES, ATTN_WIDTH)),
                     _resident(attn_gain.shape),
                     pl.BlockSpec((ATTN_WIDTH, D_MODEL), lambda i: (0, 0), pipeline_mode=pl.Buffered(1)),
                     pl.BlockSpec((GLA_V_WIDTH, D_MODEL), lambda i: (ATTN_WIDTH // GLA_V_WIDTH, 0),
                                  pipeline_mode=pl.Buffered(1)),
                     _resident(mix_post.shape), _resident(ffn_pre.shape),
                     _resident((D_MODEL, D_FF)), _resident((D_MODEL, D_FF)),
                     _resident((D_FF, D_MODEL)), _resident(ffn_post.shape)]),
        out_specs=_rows(tm, D_MODEL),
        out_shape=jax.ShapeDtypeStruct((S, D_MODEL), F32),
        scratch_shapes=[pltpu.VMEM((2, ATTN_SLABS, tm, LANES), F32),
                        pltpu.VMEM((len(DILATIONS), 2, 1, tm, LANES), F32)],
        compiler_params=pltpu.CompilerParams(dimension_semantics=("arbitrary",),
                                             vmem_limit_bytes=_vmem_limit(resident, streamed)),
        name="mixffn",
    )(*os_, *stats, g_out, h, jnp.asarray(expand, BF16), attn_gain, w_out, w_out, mix_post, ffn_pre,
      w_gate, w_up, w_down, ffn_post)


def _rope_inv_freq():
    inv = np.zeros((1, LANES), np.float32)
    freqs = np.asarray(ROPE_THETA, np.float32) ** (-(np.arange(ROPE_HALF, dtype=np.float32) * 2.0) / ROPE_DIM)
    for lane in range(LANES):
        if lane % HEAD_DIM < ROPE_DIM:
            inv[0, lane] = freqs[lane % ROPE_HALF]
    return inv


def kernel(x, mix_pre_norm, mix_post_norm, ffn_pre_norm, ffn_post_norm, w_in, gla_w_gate_up, gla_b_gate,
           gla_out_norm, attn_out_norm, w_out, w_gate, w_up, w_down):
    B, S, D = x.shape
    depth = mix_pre_norm.shape[0]
    assert D == D_MODEL and S % (max(DILATIONS) * ATTN_SUB) == 0
    assert S % GLA_BLOCK == 0 and S % ROW_TILE == 0 and S % INPROJ_TILE == 0
    assert DILATIONS[0] == 1 and all(b % a == 0 for a, b in zip(DILATIONS, DILATIONS[1:]))
    inv_freq = _rope_inv_freq()
    rope_offsets = _rope_offsets(inv_freq, INPROJ_TILE)
    w_in_b = jnp.swapaxes(w_in, 1, 2).astype(BF16)

    outs = []
    for b in range(B):
        h = x[b]
        for l in range(depth):
            *qkvs, gpack, glog = _inproj(h, l, mix_pre_norm, w_in_b, gla_w_gate_up, gla_b_gate, inv_freq, rope_offsets)
            pats = [_dilattn(qkv, d) for qkv, d in zip(qkvs, DILATIONS)]
            g_out, w_out_b, w_gate_b, w_up_b, w_down_b = _gla(gpack, glog, l, gla_out_norm,
                                                               (w_out, w_gate, w_up, w_down))
            h = _mixffn([p[0] for p in pats], [p[1] for p in pats], g_out, h, l, attn_out_norm, w_out_b,
                        mix_post_norm, ffn_pre_norm, w_gate_b, w_up_b, w_down_b, ffn_post_norm)
        outs.append(h)
    return jnp.stack(outs, axis=0)
```
